```python
import math
import jax
import jax.numpy as jnp
from jax import lax
import numpy as np


D_MODEL = 2048
BATCH = 2
SEQ = 8192
DEPTH = 2

HEAD_DIM = 64
N_HEADS_A = D_MODEL // (2 * HEAD_DIM)
N_HEADS_B = D_MODEL // (2 * HEAD_DIM)
N_HEADS_C = D_MODEL // (2 * HEAD_DIM)
QUERY_BLOCK = 128
DILATED_PATTERNS = ((128, 1), (512, 4), (2048, 16))
ALIBI_MAX_EXP = 8.0
SSM_HEAD_DIM = 64
SSM_HEADS = D_MODEL // (2 * SSM_HEAD_DIM)
SSM_D_INNER = SSM_HEADS * SSM_HEAD_DIM
SSM_GROUPS = 4
SSM_STATE = 128
CONV_WIDTH = 4
SSD_CHUNK = 128
CONV_CH = SSM_D_INNER + 2 * SSM_GROUPS * SSM_STATE
D_FF = 4 * D_MODEL
AB_IN = 3 * (N_HEADS_A + N_HEADS_B) * HEAD_DIM
AB_OUT = (N_HEADS_A + N_HEADS_B) * HEAD_DIM
CD_IN = 3 * N_HEADS_C * HEAD_DIM + N_HEADS_C + SSM_D_INNER + CONV_CH + SSM_HEADS
CD_OUT = N_HEADS_C * HEAD_DIM + SSM_D_INNER
N_EVEN = (DEPTH + 1) // 2
N_ODD = DEPTH // 2
NORM_EPS = 1e-6
NEG_INF = -1e30
FORGET_BIAS_MEAN = 3.0

kernel_name = 'hybrid_sb_dilated_fox_ssd_trunk'


def _rmsnorm(x, w):
    xf = x.astype(jnp.float32)
    y = xf * lax.rsqrt(jnp.mean(xf * xf, axis=-1, keepdims=True) + NORM_EPS)
    return (y * w.astype(jnp.float32)).astype(x.dtype)


def _split(t, sizes):
    cuts = [int(c) for c in np.cumsum(sizes)[:-1]]
    return jnp.split(t, cuts, axis=-1)


def _to_heads(t, n_heads):
    b, s, _ = t.shape
    return t.reshape(b, s, n_heads, -1).transpose(0, 2, 1, 3)


def _from_heads(t):
    b, h, s, d = t.shape
    return t.transpose(0, 2, 1, 3).reshape(b, s, h * d)


def _sweep_query_blocks(block_fn, q):
    b, h, s, d = q.shape
    out = lax.map(block_fn, jnp.arange(s // QUERY_BLOCK))
    return jnp.moveaxis(out, 0, 2).reshape(b, h, s, out.shape[-1])


def _block_queries(q, i):
    t0 = i * QUERY_BLOCK
    qb = lax.dynamic_slice_in_dim(q, t0, QUERY_BLOCK, axis=2)
    return qb, t0 + jnp.arange(QUERY_BLOCK)


def _stick_breaking_attention(q, k, v):
    seq = q.shape[2]
    scale = HEAD_DIM ** -0.5
    key_pos = jnp.arange(seq)

    def block(i):
        qb, t = _block_queries(q, i)
        z = jnp.einsum('bhqd,bhkd->bhqk', qb, k).astype(jnp.float32) * scale
        strict = key_pos[None, :] < t[:, None]
        log_keep = jnp.where(strict, jax.nn.log_sigmoid(-z), 0.0)
        later = lax.cumsum(log_keep, axis=3, reverse=True) - log_keep
        weight = jnp.where(strict, jnp.exp(jax.nn.log_sigmoid(z) + later), 0.0)
        return jnp.einsum('bhqk,bhkd->bhqd', weight.astype(v.dtype), v)

    return _sweep_query_blocks(block, q)


def _alibi_slopes(n_heads):
    return jnp.exp2(-ALIBI_MAX_EXP * jnp.arange(1, n_heads + 1, dtype=jnp.float32) / n_heads)


def _dilated_window_attention(q, k, v):
    slopes = _alibi_slopes(q.shape[1])[None, :, None, None]
    scale = HEAD_DIM ** -0.5

    def block(i):
        qb, t = _block_queries(q, i)
        lses, outs = [], []
        for window, dilation in DILATED_PATTERNS:
            dist = dilation * jnp.arange(window // dilation + 1)
            idx = t[:, None] - dist[None, :]
            valid = idx >= 0
            idx = jnp.maximum(idx, 0)
            kg = jnp.take(k, idx, axis=2)
            vg = jnp.take(v, idx, axis=2)
            logits = (jnp.einsum('bhqd,bhqjd->bhqj', qb, kg).astype(jnp.float32) * scale
                      - slopes * dist.astype(jnp.float32))
            logits = jnp.where(valid, logits, NEG_INF)
            lse = jax.nn.logsumexp(logits, axis=-1, keepdims=True)
            p = jnp.exp(logits - lse)
            lses.append(lse)
            outs.append(jnp.einsum('bhqj,bhqjd->bhqd', p.astype(v.dtype), vg).astype(jnp.float32))
        alpha = jax.nn.softmax(jnp.stack(lses), axis=0)
        return jnp.sum(alpha * jnp.stack(outs), axis=0).astype(v.dtype)

    return _sweep_query_blocks(block, q)


def _forgetting_attention(q, k, v, log_f_cum):
    seq = q.shape[2]
    scale = HEAD_DIM ** -0.5
    key_pos = jnp.arange(seq)

    def block(i):
        qb, t = _block_queries(q, i)
        f_q = lax.dynamic_slice_in_dim(log_f_cum, i * QUERY_BLOCK, QUERY_BLOCK, axis=2)
        logits = (jnp.einsum('bhqd,bhkd->bhqk', qb, k).astype(jnp.float32) * scale
                  + f_q[..., :, None] - log_f_cum[..., None, :])
        logits = jnp.where(key_pos[None, :] <= t[:, None], logits, NEG_INF)
        p = jax.nn.softmax(logits, axis=-1)
        return jnp.einsum('bhqk,bhkd->bhqd', p.astype(v.dtype), v)

    return _sweep_query_blocks(block, q)


def _causal_depthwise_conv(x, w, b):
    y = lax.conv_general_dilated(x, w[:, None, :], window_strides=(1,),
                                 padding=[(CONV_WIDTH - 1, 0)],
                                 dimension_numbers=('NWC', 'WIO', 'NWC'),
                                 feature_group_count=x.shape[-1])
    return y + b


def _ssd_chunked(xh, dt, a_neg, b_mat, c_mat):
    bsz, seqlen, n_heads, p = xh.shape
    g, n = b_mat.shape[-2:]
    e = n_heads // g
    q = SSD_CHUNK
    nc = seqlen // q
    f32 = jnp.float32
    x = xh.astype(f32).reshape(bsz, nc, q, g, e, p)
    dtc = dt.reshape(bsz, nc, q, g, e)
    bc = b_mat.astype(f32).reshape(bsz, nc, q, g, n)
    cc = c_mat.astype(f32).reshape(bsz, nc, q, g, n)
    a_cum = jnp.cumsum(dtc * a_neg.reshape(g, e), axis=2)
    xdt = x * dtc[..., None]
    seg = a_cum[:, :, :, None] - a_cum[:, :, None, :]
    causal = jnp.tril(jnp.ones((q, q), dtype=bool))[None, None, :, :, None, None]
    decay = jnp.exp(jnp.where(causal, seg, NEG_INF))
    cb = jnp.einsum('bclgn,bcsgn->bclsg', cc, bc)
    y_diag = jnp.einsum('bclsge,bcsgep->bclgep', cb[..., None] * decay, xdt)
    decay_to_end = jnp.exp(a_cum[:, :, -1:] - a_cum)
    states = jnp.einsum('bclgn,bclgep->bcgepn', bc, xdt * decay_to_end[..., None])
    chunk_decay = jnp.exp(a_cum[:, :, -1])

    def step(h, inp):
        dec, st = inp
        return dec[..., None, None] * h + st, h

    h0 = jnp.zeros((bsz, g, e, p, n), f32)
    _, h_in = lax.scan(step, h0, (jnp.moveaxis(chunk_decay, 1, 0), jnp.moveaxis(states, 1, 0)))
    h_in = jnp.moveaxis(h_in, 0, 1)
    y_off = jnp.einsum('bclgn,bcgepn->bclgep', cc, h_in) * jnp.exp(a_cum)[..., None]
    return (y_diag + y_off).reshape(bsz, seqlen, n_heads, p)


def _gated_rmsnorm(y, z, w):
    gated = y.astype(jnp.float32) * jax.nn.silu(z.astype(jnp.float32))
    shape = gated.shape
    gg = gated.reshape(shape[:-1] + (SSM_GROUPS, shape[-1] // SSM_GROUPS))
    gg = gg * lax.rsqrt(jnp.mean(gg * gg, axis=-1, keepdims=True) + NORM_EPS)
    return gg.reshape(shape) * w.astype(jnp.float32)


def _even_mixer(h, w_in, w_out):
    proj = h @ w_in
    qa, ka, va, qb, kb, vb = _split(proj, [N_HEADS_A * HEAD_DIM] * 3 + [N_HEADS_B * HEAD_DIM] * 3)
    oa = _stick_breaking_attention(_to_heads(qa, N_HEADS_A), _to_heads(ka, N_HEADS_A), _to_heads(va, N_HEADS_A))
    ob = _dilated_window_attention(_to_heads(qb, N_HEADS_B), _to_heads(kb, N_HEADS_B), _to_heads(vb, N_HEADS_B))
    return jnp.concatenate([_from_heads(oa), _from_heads(ob)], axis=-1) @ w_out


def _odd_mixer(h, w_in, b_f, conv_w, conv_b, dt_bias, a_log, d_skip, gate_norm, w_out):
    bsz, seq, _ = h.shape
    proj = h @ w_in
    qc, kc, vc, f_raw, z, xbc, dt_raw = _split(
        proj, [N_HEADS_C * HEAD_DIM] * 3 + [N_HEADS_C, SSM_D_INNER, CONV_CH, SSM_HEADS])
    log_f = jax.nn.log_sigmoid((f_raw + b_f).astype(jnp.float32))
    log_f_cum = jnp.cumsum(log_f, axis=1).transpose(0, 2, 1)
    oc = _forgetting_attention(_to_heads(qc, N_HEADS_C), _to_heads(kc, N_HEADS_C),
                               _to_heads(vc, N_HEADS_C), log_f_cum)
    xbc = jax.nn.silu(_causal_depthwise_conv(xbc, conv_w, conv_b))
    xs, b_mat, c_mat = _split(xbc, [SSM_D_INNER, SSM_GROUPS * SSM_STATE, SSM_GROUPS * SSM_STATE])
    xh = xs.reshape(bsz, seq, SSM_HEADS, SSM_HEAD_DIM)
    dt = jax.nn.softplus((dt_raw + dt_bias).astype(jnp.float32))
    a_neg = -jnp.exp(a_log.astype(jnp.float32))
    y = _ssd_chunked(xh, dt, a_neg,
                     b_mat.reshape(bsz, seq, SSM_GROUPS, SSM_STATE),
                     c_mat.reshape(bsz, seq, SSM_GROUPS, SSM_STATE))
    y = y + d_skip.astype(jnp.float32)[:, None] * xh.astype(jnp.float32)
    y = _gated_rmsnorm(y.reshape(bsz, seq, SSM_D_INNER), z, gate_norm).astype(h.dtype)
    return jnp.concatenate([_from_heads(oc), y], axis=-1) @ w_out


def _sq_relu_mlp(h, w_up, w_down):
    return jnp.square(jax.nn.relu(h @ w_up)) @ w_down


def setup_inputs(seed: int = 0) -> dict:
    key = jax.random.key(seed)
    ks = jax.random.split(key, 18)
    f32 = jnp.float32

    def normal(k, shape, scale):
        return jax.random.normal(k, shape, f32) * scale

    def gain(k, shape):
        return 1.0 + 0.05 * jax.random.normal(k, shape, f32)

    dt0 = jnp.exp(jax.random.uniform(ks[11], (N_ODD, SSM_HEADS), f32, math.log(1e-3), math.log(1e-1)))
    return {
        'x': normal(ks[0], (BATCH, SEQ, D_MODEL), 1.0),
        'mix_norm_pre': gain(ks[1], (DEPTH, D_MODEL)),
        'mix_norm_post': gain(ks[2], (DEPTH, D_MODEL)),
        'mlp_norm_pre': gain(ks[3], (DEPTH, D_MODEL)),
        'mlp_norm_post': gain(ks[4], (DEPTH, D_MODEL)),
        'ab_w_in': normal(ks[5], (N_EVEN, D_MODEL, AB_IN), D_MODEL ** -0.5),
        'ab_w_out': normal(ks[6], (N_EVEN, AB_OUT, D_MODEL), AB_OUT ** -0.5),
        'cd_w_in': normal(ks[7], (N_ODD, D_MODEL, CD_IN), D_MODEL ** -0.5),
        'cd_b_f': FORGET_BIAS_MEAN + 0.5 * jax.random.normal(ks[8], (N_ODD, N_HEADS_C), f32),
        'cd_conv_w': normal(ks[9], (N_ODD, CONV_WIDTH, CONV_CH), CONV_WIDTH ** -0.5),
        'cd_conv_b': normal(ks[10], (N_ODD, CONV_CH), 0.01),
        'cd_dt_bias': dt0 + jnp.log(-jnp.expm1(-dt0)),
        'cd_a_log': jnp.log(jax.random.uniform(ks[12], (N_ODD, SSM_HEADS), f32, 1.0, 16.0)),
        'cd_d_skip': 1.0 + 0.1 * jax.random.normal(ks[13], (N_ODD, SSM_HEADS), f32),
        'cd_gate_norm': gain(ks[14], (N_ODD, SSM_D_INNER)),
        'cd_w_out': normal(ks[15], (N_ODD, CD_OUT, D_MODEL), CD_OUT ** -0.5),
        'mlp_w_up': normal(ks[16], (DEPTH, D_MODEL, D_FF), D_MODEL ** -0.5),
        'mlp_w_down': normal(ks[17], (DEPTH, D_FF, D_MODEL), D_FF ** -0.5),
    }


def reference(x, mix_norm_pre, mix_norm_post, mlp_norm_pre, mlp_norm_post, ab_w_in, ab_w_out,
              cd_w_in, cd_b_f, cd_conv_w, cd_conv_b, cd_dt_bias, cd_a_log, cd_d_skip,
              cd_gate_norm, cd_w_out, mlp_w_up, mlp_w_down):
    for layer in range(DEPTH):
        i = layer // 2
        h = _rmsnorm(x, mix_norm_pre[layer])
        if layer % 2 == 0:
            m = _even_mixer(h, ab_w_in[i], ab_w_out[i])
        else:
            m = _odd_mixer(h, cd_w_in[i], cd_b_f[i], cd_conv_w[i], cd_conv_b[i], cd_dt_bias[i],
                           cd_a_log[i], cd_d_skip[i], cd_gate_norm[i], cd_w_out[i])
        x = x + _rmsnorm(m, mix_norm_post[layer])
        h = _rmsnorm(x, mlp_norm_pre[layer])
        x = x + _rmsnorm(_sq_relu_mlp(h, mlp_w_up[layer], mlp_w_down[layer]), mlp_norm_post[layer])
    return x
```

```python
import functools
import math

import jax
import jax.numpy as jnp
from jax import lax
from jax.experimental import pallas as pl
from jax.experimental.pallas import tpu as pltpu

F32 = jnp.float32
BF16 = jnp.bfloat16

LANES = 128
HEAD_DIM = 64
PAIR = 2 * HEAD_DIM
NORM_EPS = 1e-6
NEG_INF = -1e30
DILATED_PATTERNS = ((128, 1), (512, 4), (2048, 16))
ALIBI_MAX_EXP = 8.0
SSM_GROUPS = 4
SSM_STATE = 128
SSD_CHUNK = 128
SB_LOG_UNDERFLOW = -100.0
VMEM_LIMIT_BYTES = 56 * 1024 * 1024


def _cparams(sem):
    return pltpu.CompilerParams(dimension_semantics=sem, vmem_limit_bytes=VMEM_LIMIT_BYTES)


def _softplus(x):
    return jnp.maximum(x, 0.0) + jnp.log1p(jnp.exp(-jnp.abs(x)))


def _split_bf16(x, parts):
    out = []
    rem = x
    for _ in range(parts):
        hi = rem.astype(BF16)
        out.append(hi)
        rem = rem - hi.astype(F32)
    return out


def _dot_split(x, m, parts):
    acc = None
    for t in _split_bf16(x, parts):
        d = jnp.dot(t, m, preferred_element_type=F32)
        acc = d if acc is None else acc + d
    return acc


def _split_dot(m, x, parts):
    acc = None
    for t in _split_bf16(x, parts):
        d = jnp.dot(m, t, preferred_element_type=F32)
        acc = d if acc is None else acc + d
    return acc


def _rmsnorm_kernel(x_ref, w_ref, o_ref):
    x = x_ref[...]
    y = x * lax.rsqrt(jnp.mean(x * x, axis=-1, keepdims=True) + NORM_EPS)
    o_ref[...] = (y * w_ref[...]).astype(o_ref.dtype)


def _rmsnorm(x2d, w, tm=512):
    t, d = x2d.shape
    return pl.pallas_call(
        _rmsnorm_kernel,
        out_shape=jax.ShapeDtypeStruct((t, d), BF16),
        grid=(t // tm,),
        in_specs=[pl.BlockSpec((tm, d), lambda i: (i, 0)),
                  pl.BlockSpec((1, d), lambda i: (0, 0))],
        out_specs=pl.BlockSpec((tm, d), lambda i: (i, 0)),
        compiler_params=_cparams(("parallel",)),
        name="rmsnorm",
    )(x2d, w.reshape(1, d))


def _mm_kernel(a_ref, b_ref, o_ref, *scratch, nk, relu2):
    def finish(acc):
        if relu2:
            acc = jnp.square(jnp.maximum(acc, 0.0))
        o_ref[...] = acc.astype(o_ref.dtype)

    if nk == 1:
        finish(jnp.dot(a_ref[...], b_ref[...], preferred_element_type=F32))
        return
    acc_ref, = scratch
    k = pl.program_id(2)

    @pl.when(k == 0)
    def _():
        acc_ref[...] = jnp.zeros_like(acc_ref)

    acc_ref[...] += jnp.dot(a_ref[...], b_ref[...], preferred_element_type=F32)

    @pl.when(k == nk - 1)
    def _():
        finish(acc_ref[...])


def _matmul(a, b, out_dtype, *, relu2=False, tm=1024, tn=1792, tk=2048):
    m, kk = a.shape
    n = b.shape[1]
    tm, tn, tk = min(tm, m), min(tn, n), min(tk, kk)
    while n % tn:
        tn -= LANES
    nk = kk // tk
    scratch = [] if nk == 1 else [pltpu.VMEM((tm, tn), F32)]
    return pl.pallas_call(
        functools.partial(_mm_kernel, nk=nk, relu2=relu2),
        out_shape=jax.ShapeDtypeStruct((m, n), out_dtype),
        grid=(m // tm, n // tn, nk),
        in_specs=[pl.BlockSpec((tm, tk), lambda i, j, k: (i, k)),
                  pl.BlockSpec((tk, tn), lambda i, j, k: (k, j))],
        out_specs=pl.BlockSpec((tm, tn), lambda i, j, k: (i, j)),
        scratch_shapes=scratch,
        compiler_params=_cparams(("parallel", "parallel", "arbitrary")),
        name="matmul",
    )(a, b)


def _mm_norm_res_kernel(a_ref, b_ref, w_ref, r_ref, o_ref, acc_ref, *, nk):
    k = pl.program_id(1)

    @pl.when(k == 0)
    def _():
        acc_ref[...] = jnp.zeros_like(acc_ref)

    acc_ref[...] += jnp.dot(a_ref[...], b_ref[...], preferred_element_type=F32)

    @pl.when(k == nk - 1)
    def _():
        y = acc_ref[...]
        y = y * lax.rsqrt(jnp.mean(y * y, axis=-1, keepdims=True) + NORM_EPS)
        o_ref[...] = r_ref[...] + y * w_ref[...]


def _matmul_norm_residual(a, b, w, resid, *, tm=512, tk=512):
    m, kk = a.shape
    n = b.shape[1]
    tm, tk = min(tm, m), min(tk, kk)
    nk = kk // tk
    return pl.pallas_call(
        functools.partial(_mm_norm_res_kernel, nk=nk),
        out_shape=jax.ShapeDtypeStruct((m, n), F32),
        grid=(m // tm, nk),
        in_specs=[pl.BlockSpec((tm, tk), lambda i, k: (i, k)),
                  pl.BlockSpec((tk, n), lambda i, k: (k, 0)),
                  pl.BlockSpec((1, n), lambda i, k: (0, 0)),
                  pl.BlockSpec((tm, n), lambda i, k: (i, 0))],
        out_specs=pl.BlockSpec((tm, n), lambda i, k: (i, 0)),
        scratch_shapes=[pltpu.VMEM((tm, n), F32)],
        compiler_params=_cparams(("parallel", "arbitrary")),
        name="matmul_norm_residual",
    )(a, b, w.reshape(1, n), resid)


def _head_masks(rows):
    lane = lax.broadcasted_iota(jnp.int32, (rows, PAIR), 1)
    return lane < HEAD_DIM


def _sb_kernel(q_ref, k_ref, v_ref, o_ref, acc_ref, c_ref, *, tq):
    qi = pl.program_id(2)
    first = _head_masks(tq)
    q = q_ref[...] * jnp.asarray(HEAD_DIM ** -0.5, BF16)
    q_heads = (jnp.where(first, q, jnp.zeros_like(q)), jnp.where(first, jnp.zeros_like(q), q))
    row = lax.broadcasted_iota(jnp.int32, (tq, tq), 0)
    col = lax.broadcasted_iota(jnp.int32, (tq, tq), 1)
    upper = jnp.where(row > col, 1.0, 0.0).astype(BF16)
    strict = col < row

    acc_ref[...] = jnp.zeros_like(acc_ref)
    c_ref[...] = jnp.zeros_like(c_ref)

    def step(j, masked):
        k0 = pl.multiple_of(j * tq, tq)
        kb = k_ref[pl.ds(k0, tq), :]
        vb = v_ref[pl.ds(k0, tq), :]
        for h in range(2):
            z = lax.dot_general(q_heads[h], kb, (((1,), (1,)), ((), ())),
                                preferred_element_type=F32)
            soft = jnp.log1p(jnp.exp(-jnp.abs(z)))
            log_beta = jnp.minimum(z, 0.0) - soft
            log_keep = -jnp.maximum(z, 0.0) - soft
            if masked:
                log_keep = jnp.where(strict, log_keep, 0.0)
            later = jnp.dot(log_keep.astype(BF16), upper, preferred_element_type=F32)
            c = c_ref[h]
            w = jnp.exp(log_beta + later + c)
            if masked:
                w = jnp.where(strict, w, 0.0)
            acc_ref[h] += jnp.dot(w.astype(BF16), vb, preferred_element_type=F32)
            c_ref[h] = c + jnp.sum(log_keep, axis=1, keepdims=True)

    step(qi, True)

    def cond(carry):
        j, live = carry
        return jnp.logical_and(j >= 0, live)

    def body(carry):
        j, _ = carry
        step(j, False)
        return j - 1, jnp.max(c_ref[...]) > SB_LOG_UNDERFLOW

    lax.while_loop(cond, body, (qi - 1, jnp.max(c_ref[...]) > SB_LOG_UNDERFLOW))
    o_ref[...] = jnp.where(first, acc_ref[0], acc_ref[1]).astype(o_ref.dtype)


def _stick_breaking(proj, bsz, seq, n_heads, q_col, k_col, v_col, tq=256):
    t = proj.shape[0]
    n_pairs = n_heads // 2
    qb, kb, vb = q_col // PAIR, k_col // PAIR, v_col // PAIR
    nq = seq // tq
    return pl.pallas_call(
        functools.partial(_sb_kernel, tq=tq),
        out_shape=jax.ShapeDtypeStruct((t, n_heads * HEAD_DIM), BF16),
        grid=(bsz, n_pairs, nq),
        in_specs=[pl.BlockSpec((tq, PAIR), lambda b, p, i: (b * nq + i, qb + p)),
                  pl.BlockSpec((seq, PAIR), lambda b, p, i: (b, kb + p)),
                  pl.BlockSpec((seq, PAIR), lambda b, p, i: (b, vb + p))],
        out_specs=pl.BlockSpec((tq, PAIR), lambda b, p, i: (b * nq + i, p)),
        scratch_shapes=[pltpu.VMEM((2, tq, PAIR), F32), pltpu.VMEM((2, tq, 1), F32)],
        compiler_params=_cparams(("parallel", "parallel", "arbitrary")),
        name="stick_breaking_attention",
    )(proj, proj, proj)


def _dilated_kernel(slope_ref, q_ref, k_ref, v_ref, o_ref, lse_ref, *, length, dilation, n_keys, tq):
    pair = pl.program_id(2)
    first = _head_masks(tq)
    span = 2 * tq
    row = lax.broadcasted_iota(jnp.int32, (tq, span), 0)
    col = lax.broadcasted_iota(jnp.int32, (tq, span), 1)
    scale = jnp.asarray(HEAD_DIM ** -0.5, BF16)

    def body(i, _):
        q0 = pl.multiple_of(i * tq, tq)
        k0 = pl.multiple_of(jnp.maximum(i - 1, 0) * tq, tq)
        q = q_ref[pl.ds(q0, tq), :] * scale
        kc = k_ref[pl.ds(k0, span), :]
        vc = v_ref[pl.ds(k0, span), :]
        hops = (q0 - k0) + row - col
        valid = jnp.logical_and(hops >= 0, hops < n_keys)
        dist = (hops * dilation).astype(F32)
        outs, lses = [], []
        for h in range(2):
            qh = jnp.where(first, q, jnp.zeros_like(q)) if h == 0 else jnp.where(first, jnp.zeros_like(q), q)
            z = lax.dot_general(qh, kc, (((1,), (1,)), ((), ())), preferred_element_type=F32)
            logits = jnp.where(valid, z - slope_ref[2 * pair + h] * dist, NEG_INF)
            m = jnp.max(logits, axis=-1, keepdims=True)
            p = jnp.exp(logits - m)
            s = jnp.sum(p, axis=-1, keepdims=True)
            outs.append(jnp.dot(p.astype(BF16), vc, preferred_element_type=F32) / s)
            lses.append(jnp.broadcast_to(m + jnp.log(s), (tq, PAIR)))
        o_ref[pl.ds(q0, tq), :] = jnp.where(first, outs[0], outs[1])
        lse_ref[pl.ds(q0, tq), :] = jnp.where(first, lses[0], lses[1])
        return 0

    lax.fori_loop(0, length // tq, body, 0)


def _dilated_branch(proj, slopes, bsz, seq, n_heads, q_col, k_col, v_col, window, dilation):
    t, c = proj.shape
    r = dilation
    length = seq // r
    n_pairs = n_heads // 2
    width = n_heads * HEAD_DIM
    cb, wb = c // PAIR, width // PAIR
    qb, kb, vb = q_col // PAIR, k_col // PAIR, v_col // PAIR
    view = proj.reshape(t // r, r * c)
    tq = window // dilation
    spec = lambda off: pl.BlockSpec((length, PAIR), lambda b, cls, p: (b, cls * cb + off + p))
    out_spec = pl.BlockSpec((length, PAIR), lambda b, cls, p: (b, cls * wb + p))
    out, lse = pl.pallas_call(
        functools.partial(_dilated_kernel, length=length, dilation=r,
                          n_keys=window // dilation + 1, tq=tq),
        out_shape=(jax.ShapeDtypeStruct((t // r, r * width), F32),
                   jax.ShapeDtypeStruct((t // r, r * width), F32)),
        grid=(bsz, r, n_pairs),
        in_specs=[pl.BlockSpec(memory_space=pltpu.SMEM), spec(qb), spec(kb), spec(vb)],
        out_specs=(out_spec, out_spec),
        compiler_params=_cparams(("parallel", "parallel", "parallel")),
        name="dilated_window_attention",
    )(slopes, view, view, view)
    return out.reshape(t, width), lse.reshape(t, width)


def _merge_kernel(*refs):
    n = (len(refs) - 1) // 2
    outs, lses, o_ref = refs[:n], refs[n:2 * n], refs[2 * n]
    ls = [r[...] for r in lses]
    m = functools.reduce(jnp.maximum, ls)
    ws = [jnp.exp(l - m) for l in ls]
    den = functools.reduce(lambda a, b: a + b, ws)
    num = functools.reduce(lambda a, b: a + b, [w * o[...] for w, o in zip(ws, outs)])
    o_ref[...] = (num / den).astype(o_ref.dtype)


def _merge_branches(outs, lses, tm=512):
    t, w = outs[0].shape
    spec = pl.BlockSpec((tm, w), lambda i: (i, 0))
    return pl.pallas_call(
        _merge_kernel,
        out_shape=jax.ShapeDtypeStruct((t, w), BF16),
        grid=(t // tm,),
        in_specs=[spec] * (2 * len(outs)),
        out_specs=spec,
        compiler_params=_cparams(("parallel",)),
        name="merge_dilated_branches",
    )(*outs, *lses)


def _logf_cumsum_kernel(f_ref, b_ref, o_ref, carry_ref, *, tb):
    @pl.when(pl.program_id(1) == 0)
    def _():
        carry_ref[...] = jnp.zeros_like(carry_ref)

    x = f_ref[...] + b_ref[...]
    log_f = jnp.minimum(x, 0.0) - jnp.log1p(jnp.exp(-jnp.abs(x)))
    row = lax.broadcasted_iota(jnp.int32, (tb, tb), 0)
    col = lax.broadcasted_iota(jnp.int32, (tb, tb), 1)
    lower = jnp.where(col <= row, 1.0, 0.0).astype(BF16)
    cum = _split_dot(lower, log_f, 3) + carry_ref[...]
    o_ref[...] = cum
    carry_ref[...] = cum[tb - 1:tb, :]


def _logf_cumsum(rest, b_f_row, bsz, seq, col_block, tb=256):
    t = rest.shape[0]
    nb = seq // tb
    return pl.pallas_call(
        functools.partial(_logf_cumsum_kernel, tb=tb),
        out_shape=jax.ShapeDtypeStruct((t, LANES), F32),
        grid=(bsz, nb),
        in_specs=[pl.BlockSpec((tb, LANES), lambda b, i: (b * nb + i, col_block)),
                  pl.BlockSpec((1, LANES), lambda b, i: (0, 0))],
        out_specs=pl.BlockSpec((tb, LANES), lambda b, i: (b * nb + i, 0)),
        scratch_shapes=[pltpu.VMEM((1, LANES), F32)],
        compiler_params=_cparams(("parallel", "arbitrary")),
        name="log_forget_cumsum",
    )(rest, b_f_row)


def _fox_kernel(q_ref, k_ref, v_ref, fq_ref, fk_ref, o_ref, acc_ref, m_ref, l_ref, *, tq):
    pair = pl.program_id(1)
    qi = pl.program_id(2)
    first = _head_masks(tq)
    q = q_ref[...] * jnp.asarray(HEAD_DIM ** -0.5, BF16)
    q_heads = (jnp.where(first, q, jnp.zeros_like(q)), jnp.where(first, jnp.zeros_like(q), q))
    fq_all = fq_ref[...]
    lane = lax.broadcasted_iota(jnp.int32, (tq, LANES), 1)
    f_q = [jnp.sum(jnp.where(lane == 2 * pair + h, fq_all, 0.0), axis=1, keepdims=True)
           for h in range(2)]
    row = lax.broadcasted_iota(jnp.int32, (tq, tq), 0)
    col = lax.broadcasted_iota(jnp.int32, (tq, tq), 1)
    causal = col <= row

    acc_ref[...] = jnp.zeros_like(acc_ref)
    m_ref[...] = jnp.full_like(m_ref, NEG_INF)
    l_ref[...] = jnp.zeros_like(l_ref)

    def step(j, masked):
        k0 = pl.multiple_of(j * tq, tq)
        kb = k_ref[pl.ds(k0, tq), :]
        vb = v_ref[pl.ds(k0, tq), :]
        for h in range(2):
            f_k = fk_ref[pl.ds(2 * pair + h, 1), pl.ds(k0, tq)]
            s = lax.dot_general(q_heads[h], kb, (((1,), (1,)), ((), ())),
                                preferred_element_type=F32)
            s = s + f_q[h] - f_k
            if masked:
                s = jnp.where(causal, s, NEG_INF)
            m_old = m_ref[h]
            m_new = jnp.maximum(m_old, jnp.max(s, axis=-1, keepdims=True))
            alpha = jnp.exp(m_old - m_new)
            p = jnp.exp(s - m_new)
            l_ref[h] = alpha * l_ref[h] + jnp.sum(p, axis=-1, keepdims=True)
            acc_ref[h] = alpha * acc_ref[h] + jnp.dot(p.astype(BF16), vb, preferred_element_type=F32)
            m_ref[h] = m_new

    def body(j, _):
        step(j, False)
        return 0

    lax.fori_loop(0, qi, body, 0)
    step(qi, True)
    o_ref[...] = jnp.where(first, acc_ref[0] / l_ref[0], acc_ref[1] / l_ref[1]).astype(o_ref.dtype)


def _forgetting_attention(qkv, f_cum, f_cum_t, bsz, seq, n_heads, tq=256):
    t = qkv.shape[0]
    n_pairs = n_heads // 2
    nq = seq // tq
    return pl.pallas_call(
        functools.partial(_fox_kernel, tq=tq),
        out_shape=jax.ShapeDtypeStruct((t, n_heads * HEAD_DIM), BF16),
        grid=(bsz, n_pairs, nq),
        in_specs=[pl.BlockSpec((tq, PAIR), lambda b, p, i: (b * nq + i, p)),
                  pl.BlockSpec((seq, PAIR), lambda b, p, i: (b, n_pairs + p)),
                  pl.BlockSpec((seq, PAIR), lambda b, p, i: (b, 2 * n_pairs + p)),
                  pl.BlockSpec((tq, LANES), lambda b, p, i: (b * nq + i, 0)),
                  pl.BlockSpec((None, n_heads, seq), lambda b, p, i: (b, 0, 0))],
        out_specs=pl.BlockSpec((tq, PAIR), lambda b, p, i: (b * nq + i, p)),
        scratch_shapes=[pltpu.VMEM((2, tq, PAIR), F32), pltpu.VMEM((2, tq, 1), F32),
                        pltpu.VMEM((2, tq, 1), F32)],
        compiler_params=_cparams(("parallel", "parallel", "arbitrary")),
        name="forgetting_attention",
    )(qkv, qkv, qkv, f_cum, f_cum_t)


def _conv_silu_kernel(x_ref, halo_ref, w_ref, b_ref, o_ref, *, tb, blocks_per_seq, width):
    i = pl.program_id(0)
    x = x_ref[...]
    halo = jnp.where(i % blocks_per_seq == 0, jnp.zeros_like(halo_ref[...]), halo_ref[...])
    xx = jnp.concatenate([halo, x], axis=0)
    w = w_ref[...]
    y = b_ref[...] + w[width - 1:width, :] * x
    for tap in range(width - 1):
        shift = width - 1 - tap
        y = y + w[tap:tap + 1, :] * xx[8 - shift:8 - shift + tb, :]
    o_ref[...] = y * (1.0 / (1.0 + jnp.exp(-y)))


def _conv_silu(rest, conv_w, conv_b, seq, col0, n_ch, tb=512):
    t = rest.shape[0]
    width = conv_w.shape[0]
    tc = math.gcd(col0, n_ch)
    cb0 = col0 // tc
    hb = tb // 8
    return pl.pallas_call(
        functools.partial(_conv_silu_kernel, tb=tb, blocks_per_seq=seq // tb, width=width),
        out_shape=jax.ShapeDtypeStruct((t, n_ch), F32),
        grid=(t // tb, n_ch // tc),
        in_specs=[pl.BlockSpec((tb, tc), lambda i, j: (i, cb0 + j)),
                  pl.BlockSpec((8, tc), lambda i, j: (jnp.maximum(i * hb - 1, 0), cb0 + j)),
                  pl.BlockSpec((width, tc), lambda i, j: (0, j)),
                  pl.BlockSpec((1, tc), lambda i, j: (0, j))],
        out_specs=pl.BlockSpec((tb, tc), lambda i, j: (i, j)),
        compiler_params=_cparams(("parallel", "parallel")),
        name="causal_conv_silu",
    )(rest, rest, conv_w, conv_b.reshape(1, n_ch))


def _ssd_kernel(xc_ref, z_ref, dt_ref, dtt_ref, bias_row_ref, bias_col_ref, alog_row_ref,
                alog_col_ref, dskip_ref, gate_ref, o_ref, state_ref, *, n_heads, d_inner):
    q = SSD_CHUNK
    n = SSM_STATE
    hpg = n_heads // SSM_GROUPS
    gw = hpg * HEAD_DIM

    @pl.when(pl.program_id(1) == 0)
    def _():
        state_ref[...] = jnp.zeros_like(state_ref)

    row = lax.broadcasted_iota(jnp.int32, (q, q), 0)
    col = lax.broadcasted_iota(jnp.int32, (q, q), 1)
    causal = col <= row
    lower = jnp.where(causal, 1.0, 0.0).astype(BF16)
    upper = jnp.where(row <= col, 1.0, 0.0).astype(BF16)
    er = lax.broadcasted_iota(jnp.int32, (LANES, d_inner), 0)
    ec = lax.broadcasted_iota(jnp.int32, (LANES, d_inner), 1)
    expand = jnp.where(ec // HEAD_DIM == er, 1.0, 0.0).astype(BF16)
    first = _head_masks(q)

    dt = _softplus(dt_ref[...] + bias_row_ref[...])
    a_cum = _split_dot(lower, dt * (-jnp.exp(alog_row_ref[...])), 3)
    a_last = a_cum[q - 1:q, :]
    dt_t = _softplus(dtt_ref[...] + bias_col_ref[...])
    a_cum_t = _dot_split(dt_t * (-jnp.exp(alog_col_ref[...])), upper, 3)

    dt_x = _dot_split(dt, expand, 2)
    decay_in_x = _dot_split(jnp.exp(a_cum), expand, 2)
    decay_out_x = _dot_split(jnp.exp(a_last - a_cum), expand, 2)
    chunk_decay_x = _dot_split(jnp.broadcast_to(jnp.exp(a_last), (8, LANES)), expand, 2)[0:1, :]

    xs = xc_ref[:, 0:d_inner]
    xdt = xs * dt_x
    xdt_bf = xdt.astype(BF16)
    xend_bf = (xdt * decay_out_x).astype(BF16)

    y_groups = []
    for g in range(SSM_GROUPS):
        b_g = xc_ref[:, d_inner + g * n:d_inner + (g + 1) * n].astype(BF16)
        c_g = xc_ref[:, d_inner + (SSM_GROUPS + g) * n:d_inner + (SSM_GROUPS + g + 1) * n].astype(BF16)
        cb = lax.dot_general(c_g, b_g, (((1,), (1,)), ((), ())), preferred_element_type=F32)
        state = state_ref[g]
        y_off = jnp.dot(c_g, state.astype(BF16), preferred_element_type=F32)
        y_g = y_off * decay_in_x[:, g * gw:(g + 1) * gw]
        diag_pairs = []
        for pr in range(hpg // 2):
            lo = g * gw + pr * PAIR
            x_pair = xdt_bf[:, lo:lo + PAIR]
            acc = None
            for hh in range(2):
                h = g * hpg + pr * 2 + hh
                seg = a_cum[:, h:h + 1] - a_cum_t[h:h + 1, :]
                m = (cb * jnp.where(causal, jnp.exp(seg), 0.0)).astype(BF16)
                x_h = jnp.where(first, x_pair, jnp.zeros_like(x_pair)) if hh == 0 else \
                    jnp.where(first, jnp.zeros_like(x_pair), x_pair)
                d = jnp.dot(m, x_h, preferred_element_type=F32)
                acc = d if acc is None else acc + d
            diag_pairs.append(acc)
        y_groups.append(y_g + jnp.concatenate(diag_pairs, axis=1))
        new_state = lax.dot_general(b_g, xend_bf[:, g * gw:(g + 1) * gw], (((0,), (0,)), ((), ())),
                                    preferred_element_type=F32)
        state_ref[g] = state * chunk_decay_x[:, g * gw:(g + 1) * gw] + new_state

    y = jnp.concatenate(y_groups, axis=1) + dskip_ref[...] * xs
    z = z_ref[...]
    gated = y * (z * (1.0 / (1.0 + jnp.exp(-z))))
    gn = d_inner // SSM_GROUPS
    normed = []
    for g in range(SSM_GROUPS):
        gg = gated[:, g * gn:(g + 1) * gn]
        normed.append(gg * lax.rsqrt(jnp.mean(gg * gg, axis=-1, keepdims=True) + NORM_EPS))
    o_ref[...] = (jnp.concatenate(normed, axis=1) * gate_ref[...]).astype(o_ref.dtype)


def _ssd(conv_out, rest, dt_t, dt_bias, a_log, d_skip, gate_norm, bsz, seq, n_heads, d_inner,
         z_col, dt_col_block):
    t = conv_out.shape[0]
    nc = seq // SSD_CHUNK
    conv_ch = conv_out.shape[1]
    pad = LANES - n_heads
    row = lambda v: jnp.pad(v, (0, pad)).reshape(1, LANES)
    colv = lambda v: v.reshape(n_heads, 1)
    zb = z_col // d_inner
    const = lambda shape: pl.BlockSpec(shape, lambda b, c: (0,) * len(shape))
    return pl.pallas_call(
        functools.partial(_ssd_kernel, n_heads=n_heads, d_inner=d_inner),
        out_shape=jax.ShapeDtypeStruct((t, d_inner), BF16),
        grid=(bsz, nc),
        in_specs=[pl.BlockSpec((SSD_CHUNK, conv_ch), lambda b, c: (b * nc + c, 0)),
                  pl.BlockSpec((SSD_CHUNK, d_inner), lambda b, c: (b * nc + c, zb)),
                  pl.BlockSpec((SSD_CHUNK, LANES), lambda b, c: (b * nc + c, dt_col_block)),
                  pl.BlockSpec((None, n_heads, SSD_CHUNK), lambda b, c: (b, 0, c)),
                  const((1, LANES)), const((n_heads, 1)), const((1, LANES)), const((n_heads, 1)),
                  const((1, d_inner)), const((1, d_inner))],
        out_specs=pl.BlockSpec((SSD_CHUNK, d_inner), lambda b, c: (b * nc + c, 0)),
        scratch_shapes=[pltpu.VMEM((SSM_GROUPS, SSM_STATE, d_inner // SSM_GROUPS), F32)],
        compiler_params=_cparams(("parallel", "arbitrary")),
        name="ssd_chunk_scan",
    )(conv_out, rest, rest, dt_t, row(dt_bias), colv(dt_bias), row(a_log), colv(a_log),
      jnp.repeat(d_skip, HEAD_DIM).reshape(1, d_inner), gate_norm.reshape(1, d_inner))


def _even_mixer(h, w_in, w_out, post_w, resid, bsz, seq):
    d = h.shape[1]
    n_heads = d // (2 * HEAD_DIM)
    hw = n_heads * HEAD_DIM
    proj = _matmul(h, w_in.astype(BF16), BF16)
    oa = _stick_breaking(proj, bsz, seq, n_heads, 0, hw, 2 * hw)
    slopes = jnp.exp2(-ALIBI_MAX_EXP * jnp.arange(1, n_heads + 1, dtype=F32) / n_heads)
    outs, lses = [], []
    for window, dilation in DILATED_PATTERNS:
        o, l = _dilated_branch(proj, slopes, bsz, seq, n_heads, 3 * hw, 4 * hw, 5 * hw,
                               window, dilation)
        outs.append(o)
        lses.append(l)
    ob = _merge_branches(outs, lses)
    mixed = jnp.concatenate([oa, ob], axis=1)
    return _matmul_norm_residual(mixed, w_out.astype(BF16), post_w, resid)


def _odd_mixer(h, w_in, b_f, conv_w, conv_b, dt_bias, a_log, d_skip, gate_norm, w_out, post_w,
               resid, bsz, seq):
    d = h.shape[1]
    n_heads = d // (2 * HEAD_DIM)
    hw = n_heads * HEAD_DIM
    d_inner = hw
    conv_ch = d_inner + 2 * SSM_GROUPS * SSM_STATE
    pad = LANES - n_heads
    c0 = 3 * hw
    w_qkv = w_in[:, :c0].astype(BF16)
    f_w = w_in[:, c0:c0 + n_heads]
    z_w = w_in[:, c0 + n_heads:c0 + n_heads + d_inner]
    xbc_w = w_in[:, c0 + n_heads + d_inner:c0 + n_heads + d_inner + conv_ch]
    dt_w = w_in[:, c0 + n_heads + d_inner + conv_ch:]
    w_rest = jnp.concatenate([z_w, xbc_w, jnp.pad(f_w, ((0, 0), (0, pad))),
                              jnp.pad(dt_w, ((0, 0), (0, pad)))], axis=1).astype(BF16)
    qkv = _matmul(h, w_qkv, BF16)
    rest = _matmul(h, w_rest, F32)
    f_block = (d_inner + conv_ch) // LANES
    dt_block = f_block + 1

    f_cum = _logf_cumsum(rest, jnp.pad(b_f, (0, pad)).reshape(1, LANES), bsz, seq, f_block)
    f_cum_t = f_cum[:, :n_heads].reshape(bsz, seq, n_heads).transpose(0, 2, 1)
    oc = _forgetting_attention(qkv, f_cum, f_cum_t, bsz, seq, n_heads)

    conv_out = _conv_silu(rest, conv_w, conv_b, seq, d_inner, conv_ch)
    dt_col = dt_block * LANES
    dt_t = rest[:, dt_col:dt_col + n_heads].reshape(bsz, seq, n_heads).transpose(0, 2, 1)
    y = _ssd(conv_out, rest, dt_t, dt_bias, a_log, d_skip, gate_norm, bsz, seq, n_heads, d_inner,
             0, dt_block)
    mixed = jnp.concatenate([oc, y], axis=1)
    return _matmul_norm_residual(mixed, w_out.astype(BF16), post_w, resid)


def _mlp(h, w_up, w_down, post_w, resid):
    hidden = _matmul(h, w_up.astype(BF16), BF16, relu2=True)
    return _matmul_norm_residual(hidden, w_down.astype(BF16), post_w, resid)


def kernel(x, mix_norm_pre, mix_norm_post, mlp_norm_pre, mlp_norm_post, ab_w_in, ab_w_out,
           cd_w_in, cd_b_f, cd_conv_w, cd_conv_b, cd_dt_bias, cd_a_log, cd_d_skip,
           cd_gate_norm, cd_w_out, mlp_w_up, mlp_w_down):
    bsz, seq, d = x.shape
    depth = mix_norm_pre.shape[0]
    xf = x.reshape(bsz * seq, d)
    for layer in range(depth):
        i = layer // 2
        h = _rmsnorm(xf, mix_norm_pre[layer])
        if layer % 2 == 0:
            xf = _even_mixer(h, ab_w_in[i], ab_w_out[i], mix_norm_post[layer], xf, bsz, seq)
        else:
            xf = _odd_mixer(h, cd_w_in[i], cd_b_f[i], cd_conv_w[i], cd_conv_b[i], cd_dt_bias[i],
                            cd_a_log[i], cd_d_skip[i], cd_gate_norm[i], cd_w_out[i],
                            mix_norm_post[layer], xf, bsz, seq)
        h = _rmsnorm(xf, mlp_norm_pre[layer])
        xf = _mlp(h, mlp_w_up[layer], mlp_w_down[layer], mlp_norm_post[layer], xf)
    return xf.reshape(bsz, seq, d)
```

```python
import functools
import math

import jax
import jax.numpy as jnp
from jax import lax
from jax.experimental import pallas as pl
from jax.experimental.pallas import tpu as pltpu

F32 = jnp.float32
BF16 = jnp.bfloat16

LANES = 128
HEAD_DIM = 64
PAIR = 2 * HEAD_DIM
NORM_EPS = 1e-6
NEG_INF = -1e30
DILATED_PATTERNS = ((128, 1), (512, 4), (2048, 16))
ALIBI_MAX_EXP = 8.0
SSM_GROUPS = 4
SSM_STATE = 128
SSD_CHUNK = 128
SB_LOG_UNDERFLOW = -100.0
VMEM_LIMIT_BYTES = 56 * 1024 * 1024


def _cparams(sem):
    return pltpu.CompilerParams(dimension_semantics=sem, vmem_limit_bytes=VMEM_LIMIT_BYTES)


def _softplus(x):
    return jnp.maximum(x, 0.0) + jnp.log1p(jnp.exp(-jnp.abs(x)))


def _split_bf16(x, parts):
    out = []
    rem = x
    for _ in range(parts):
        hi = rem.astype(BF16)
        out.append(hi)
        rem = rem - hi.astype(F32)
    return out


def _dot_split(x, m, parts):
    acc = None
    for t in _split_bf16(x, parts):
        d = jnp.dot(t, m, preferred_element_type=F32)
        acc = d if acc is None else acc + d
    return acc


def _split_dot(m, x, parts):
    acc = None
    for t in _split_bf16(x, parts):
        d = jnp.dot(m, t, preferred_element_type=F32)
        acc = d if acc is None else acc + d
    return acc


def _rmsnorm_kernel(x_ref, w_ref, o_ref):
    x = x_ref[...]
    y = x * lax.rsqrt(jnp.mean(x * x, axis=-1, keepdims=True) + NORM_EPS)
    o_ref[...] = (y * w_ref[...]).astype(o_ref.dtype)


def _rmsnorm(x2d, w, tm=512):
    t, d = x2d.shape
    return pl.pallas_call(
        _rmsnorm_kernel,
        out_shape=jax.ShapeDtypeStruct((t, d), BF16),
        grid=(t // tm,),
        in_specs=[pl.BlockSpec((tm, d), lambda i: (i, 0)),
                  pl.BlockSpec((1, d), lambda i: (0, 0))],
        out_specs=pl.BlockSpec((tm, d), lambda i: (i, 0)),
        compiler_params=_cparams(("parallel",)),
        name="rmsnorm",
    )(x2d, w.reshape(1, d))


def _mm_kernel(a_ref, b_ref, o_ref, *scratch, nk, relu2):
    def finish(acc):
        if relu2:
            acc = jnp.square(jnp.maximum(acc, 0.0))
        o_ref[...] = acc.astype(o_ref.dtype)

    if nk == 1:
        finish(jnp.dot(a_ref[...], b_ref[...], preferred_element_type=F32))
        return
    acc_ref, = scratch
    k = pl.program_id(2)

    @pl.when(k == 0)
    def _():
        acc_ref[...] = jnp.zeros_like(acc_ref)

    acc_ref[...] += jnp.dot(a_ref[...], b_ref[...], preferred_element_type=F32)

    @pl.when(k == nk - 1)
    def _():
        finish(acc_ref[...])


def _matmul(a, b, out_dtype, *, relu2=False, tm=1024, tn=1792, tk=2048):
    m, kk = a.shape
    n = b.shape[1]
    tm, tn, tk = min(tm, m), min(tn, n), min(tk, kk)
    while n % tn:
        tn -= LANES
    nk = kk // tk
    scratch = [] if nk == 1 else [pltpu.VMEM((tm, tn), F32)]
    return pl.pallas_call(
        functools.partial(_mm_kernel, nk=nk, relu2=relu2),
        out_shape=jax.ShapeDtypeStruct((m, n), out_dtype),
        grid=(m // tm, n // tn, nk),
        in_specs=[pl.BlockSpec((tm, tk), lambda i, j, k: (i, k)),
                  pl.BlockSpec((tk, tn), lambda i, j, k: (k, j))],
        out_specs=pl.BlockSpec((tm, tn), lambda i, j, k: (i, j)),
        scratch_shapes=scratch,
        compiler_params=_cparams(("parallel", "parallel", "arbitrary")),
        name="matmul",
    )(a, b)


def _mm_norm_res_kernel(a_ref, b_ref, w_ref, r_ref, o_ref, *, nk, n_chunk, m_chunk):
    k = pl.program_id(1)
    tm, n = o_ref.shape

    @pl.when(k == 0)
    def _():
        o_ref[...] = jnp.zeros_like(o_ref)

    a = a_ref[...]
    for n0 in range(0, n, n_chunk):
        o_ref[:, n0:n0 + n_chunk] += jnp.dot(a, b_ref[:, n0:n0 + n_chunk],
                                             preferred_element_type=F32)

    @pl.when(k == nk - 1)
    def _():
        for m0 in range(0, tm, m_chunk):
            y = o_ref[m0:m0 + m_chunk, :]
            y = y * lax.rsqrt(jnp.mean(y * y, axis=-1, keepdims=True) + NORM_EPS)
            o_ref[m0:m0 + m_chunk, :] = r_ref[m0:m0 + m_chunk, :] + y * w_ref[...]


def _matmul_norm_residual(a, b, w, resid, *, tm=1024, tk=512):
    m, kk = a.shape
    n = b.shape[1]
    tm, tk = min(tm, m), min(tk, kk)
    nk = kk // tk
    return pl.pallas_call(
        functools.partial(_mm_norm_res_kernel, nk=nk, n_chunk=min(n, 512), m_chunk=min(tm, 256)),
        out_shape=jax.ShapeDtypeStruct((m, n), F32),
        grid=(m // tm, nk),
        in_specs=[pl.BlockSpec((tm, tk), lambda i, k: (i, k)),
                  pl.BlockSpec((tk, n), lambda i, k: (k, 0)),
                  pl.BlockSpec((1, n), lambda i, k: (0, 0)),
                  pl.BlockSpec((tm, n), lambda i, k: (i, 0))],
        out_specs=pl.BlockSpec((tm, n), lambda i, k: (i, 0)),
        compiler_params=_cparams(("parallel", "arbitrary")),
        name="matmul_norm_residual",
    )(a, b, w.reshape(1, n), resid)


def _head_masks(rows):
    lane = lax.broadcasted_iota(jnp.int32, (rows, PAIR), 1)
    return lane < HEAD_DIM


def _sb_kernel(q_ref, k_ref, v_ref, o_ref, acc_ref, c_ref, *, tq):
    qi = pl.program_id(2)
    first = _head_masks(tq)
    q = q_ref[...] * jnp.asarray(HEAD_DIM ** -0.5, BF16)
    q_heads = (jnp.where(first, q, jnp.zeros_like(q)), jnp.where(first, jnp.zeros_like(q), q))
    row = lax.broadcasted_iota(jnp.int32, (tq, tq), 0)
    col = lax.broadcasted_iota(jnp.int32, (tq, tq), 1)
    upper = jnp.where(row > col, 1.0, 0.0).astype(BF16)
    strict = col < row

    acc_ref[...] = jnp.zeros_like(acc_ref)
    c_ref[...] = jnp.zeros_like(c_ref)

    def step(j, masked):
        k0 = pl.multiple_of(j * tq, tq)
        kb = k_ref[pl.ds(k0, tq), :]
        vb = v_ref[pl.ds(k0, tq), :]
        for h in range(2):
            z = lax.dot_general(q_heads[h], kb, (((1,), (1,)), ((), ())),
                                preferred_element_type=F32)
            soft = jnp.log1p(jnp.exp(-jnp.abs(z)))
            log_beta = jnp.minimum(z, 0.0) - soft
            log_keep = -jnp.maximum(z, 0.0) - soft
            if masked:
                log_keep = jnp.where(strict, log_keep, 0.0)
            later = jnp.dot(log_keep.astype(BF16), upper, preferred_element_type=F32)
            c = c_ref[h]
            w = jnp.exp(log_beta + later + c)
            if masked:
                w = jnp.where(strict, w, 0.0)
            acc_ref[h] += jnp.dot(w.astype(BF16), vb, preferred_element_type=F32)
            c_ref[h] = c + jnp.sum(log_keep, axis=1, keepdims=True)

    step(qi, True)

    def cond(carry):
        j, live = carry
        return jnp.logical_and(j >= 0, live)

    def body(carry):
        j, _ = carry
        step(j, False)
        return j - 1, jnp.max(c_ref[...]) > SB_LOG_UNDERFLOW

    lax.while_loop(cond, body, (qi - 1, jnp.max(c_ref[...]) > SB_LOG_UNDERFLOW))
    o_ref[...] = jnp.where(first, acc_ref[0], acc_ref[1]).astype(o_ref.dtype)


def _stick_breaking(proj, bsz, seq, n_heads, q_col, k_col, v_col, tq=256):
    t = proj.shape[0]
    n_pairs = n_heads // 2
    qb, kb, vb = q_col // PAIR, k_col // PAIR, v_col // PAIR
    nq = seq // tq
    return pl.pallas_call(
        functools.partial(_sb_kernel, tq=tq),
        out_shape=jax.ShapeDtypeStruct((t, n_heads * HEAD_DIM), BF16),
        grid=(bsz, n_pairs, nq),
        in_specs=[pl.BlockSpec((tq, PAIR), lambda b, p, i: (b * nq + i, qb + p)),
                  pl.BlockSpec((seq, PAIR), lambda b, p, i: (b, kb + p)),
                  pl.BlockSpec((seq, PAIR), lambda b, p, i: (b, vb + p))],
        out_specs=pl.BlockSpec((tq, PAIR), lambda b, p, i: (b * nq + i, p)),
        scratch_shapes=[pltpu.VMEM((2, tq, PAIR), F32), pltpu.VMEM((2, tq, 1), F32)],
        compiler_params=_cparams(("parallel", "parallel", "arbitrary")),
        name="stick_breaking_attention",
    )(proj, proj, proj)


def _dilated_kernel(slope_ref, q_ref, k_ref, v_ref, o_ref, num_ref, den_ref, max_ref, *, seq, tq):
    pair = pl.program_id(1)
    first = _head_masks(tq)
    span = 2 * tq
    first_keys = _head_masks(span)
    ones = jnp.ones((span, PAIR), BF16)
    row = lax.broadcasted_iota(jnp.int32, (tq, span), 0)
    col = lax.broadcasted_iota(jnp.int32, (tq, span), 1)
    scale = HEAD_DIM ** -0.5

    for branch, (window, r) in enumerate(DILATED_PATTERNS):
        assert window // r == tq
        per_class = seq // (r * tq)
        stride = None if r == 1 else r

        def bias(offset, h):
            hops = offset + row - col
            valid = jnp.logical_and(hops >= 0, hops <= tq)
            return jnp.where(valid, -(slope_ref[2 * pair + h] * (hops * r).astype(F32)), NEG_INF)

        bias_first = [bias(0, h) for h in range(2)]
        bias_rest = [bias(tq, h) for h in range(2)]

        def body(n, _):
            cls = n // per_class
            i = n - cls * per_class
            m0 = i * tq
            k0 = jnp.maximum(m0 - tq, 0)
            q_rows = pl.ds(cls + r * m0, tq, stride=stride)
            k_rows = pl.ds(cls + r * k0, span, stride=stride)
            q = (q_ref[q_rows, :] * scale).astype(BF16)
            kc = k_ref[k_rows, :].astype(BF16)
            vc = v_ref[k_rows, :].astype(BF16)
            q_heads = (jnp.where(first, q, jnp.zeros_like(q)), jnp.where(first, jnp.zeros_like(q), q))
            v_heads = (jnp.where(first_keys, vc, ones), jnp.where(first_keys, ones, vc))
            res, maxes = [], []
            for h in range(2):
                z = lax.dot_general(q_heads[h], kc, (((1,), (1,)), ((), ())),
                                    preferred_element_type=F32)
                logits = z + jnp.where(i == 0, bias_first[h], bias_rest[h])
                m = jnp.max(logits, axis=-1, keepdims=True)
                p = jnp.exp(logits - m)
                res.append(jnp.dot(p.astype(BF16), v_heads[h], preferred_element_type=F32))
                maxes.append(jnp.broadcast_to(m, (tq, PAIR)))
            num = jnp.where(first, res[0], res[1])
            den = pltpu.roll(jnp.where(first, res[1], res[0]), HEAD_DIM, 1)
            mx = jnp.where(first, maxes[0], maxes[1])
            if branch == 0:
                num_ref[q_rows, :] = num
                den_ref[q_rows, :] = den
                max_ref[q_rows, :] = mx
            else:
                m_old = max_ref[q_rows, :]
                m_new = jnp.maximum(m_old, mx)
                w_old = jnp.exp(m_old - m_new)
                w_new = jnp.exp(mx - m_new)
                num_ref[q_rows, :] = w_old * num_ref[q_rows, :] + w_new * num
                den_ref[q_rows, :] = w_old * den_ref[q_rows, :] + w_new * den
                max_ref[q_rows, :] = m_new
            return 0

        lax.fori_loop(0, seq // tq, body, 0, unroll=8)

    o_ref[...] = (num_ref[...] / den_ref[...]).astype(o_ref.dtype)


def _dilated_attention(qkv, slopes, bsz, seq, n_heads):
    t = qkv.shape[0]
    n_pairs = n_heads // 2
    tq = DILATED_PATTERNS[0][0] // DILATED_PATTERNS[0][1]
    spec = lambda off: pl.BlockSpec((seq, PAIR), lambda b, p: (b, off + p))
    return pl.pallas_call(
        functools.partial(_dilated_kernel, seq=seq, tq=tq),
        out_shape=jax.ShapeDtypeStruct((t, n_heads * HEAD_DIM), BF16),
        grid=(bsz, n_pairs),
        in_specs=[pl.BlockSpec(memory_space=pltpu.SMEM), spec(0), spec(n_pairs), spec(2 * n_pairs)],
        out_specs=pl.BlockSpec((seq, PAIR), lambda b, p: (b, p)),
        scratch_shapes=[pltpu.VMEM((seq, PAIR), F32)] * 3,
        compiler_params=_cparams(("parallel", "parallel")),
        name="dilated_window_attention",
    )(slopes, qkv, qkv, qkv)


def _logf_cumsum_kernel(f_ref, b_ref, o_ref, carry_ref, *, tb):
    @pl.when(pl.program_id(1) == 0)
    def _():
        carry_ref[...] = jnp.zeros_like(carry_ref)

    x = f_ref[...] + b_ref[...]
    log_f = jnp.minimum(x, 0.0) - jnp.log1p(jnp.exp(-jnp.abs(x)))
    row = lax.broadcasted_iota(jnp.int32, (tb, tb), 0)
    col = lax.broadcasted_iota(jnp.int32, (tb, tb), 1)
    lower = jnp.where(col <= row, 1.0, 0.0).astype(BF16)
    cum = _split_dot(lower, log_f, 3) + carry_ref[...]
    o_ref[...] = cum
    carry_ref[...] = cum[tb - 1:tb, :]


def _logf_cumsum(rest, b_f_row, bsz, seq, col_block, tb=256):
    t = rest.shape[0]
    nb = seq // tb
    return pl.pallas_call(
        functools.partial(_logf_cumsum_kernel, tb=tb),
        out_shape=jax.ShapeDtypeStruct((t, LANES), F32),
        grid=(bsz, nb),
        in_specs=[pl.BlockSpec((tb, LANES), lambda b, i: (b * nb + i, col_block)),
                  pl.BlockSpec((1, LANES), lambda b, i: (0, 0))],
        out_specs=pl.BlockSpec((tb, LANES), lambda b, i: (b * nb + i, 0)),
        scratch_shapes=[pltpu.VMEM((1, LANES), F32)],
        compiler_params=_cparams(("parallel", "arbitrary")),
        name="log_forget_cumsum",
    )(rest, b_f_row)


def _fox_kernel(q_ref, k_ref, v_ref, fq_ref, fk_ref, o_ref, acc_ref, m_ref, *, tq):
    pair = pl.program_id(1)
    qi = pl.program_id(2)
    reps = tq // LANES
    first = _head_masks(tq)
    q = q_ref[...] * jnp.asarray(HEAD_DIM ** -0.5, BF16)
    q_heads = (jnp.where(first, q, jnp.zeros_like(q)), jnp.where(first, jnp.zeros_like(q), q))
    fq_all = fq_ref[...]
    lane = lax.broadcasted_iota(jnp.int32, (tq, LANES), 1)
    f_q = [jnp.broadcast_to(jnp.sum(jnp.where(lane == 2 * pair + h, fq_all, 0.0), axis=1,
                                    keepdims=True), (tq, LANES)) for h in range(2)]
    row = lax.broadcasted_iota(jnp.int32, (tq, tq), 0)
    col = lax.broadcasted_iota(jnp.int32, (tq, tq), 1)
    causal = col <= row
    ones = jnp.ones((tq, PAIR), BF16)

    acc_ref[...] = jnp.zeros_like(acc_ref)
    m_ref[...] = jnp.full_like(m_ref, NEG_INF)

    def step(j, masked):
        k0 = pl.multiple_of(j * tq, tq)
        kb = k_ref[pl.ds(k0, tq), :]
        vb = v_ref[pl.ds(k0, tq), :]
        v_heads = (jnp.where(first, vb, ones), jnp.where(first, ones, vb))
        for h in range(2):
            f_k = fk_ref[pl.ds(2 * pair + h, 1), pl.ds(k0, tq)]
            s = lax.dot_general(q_heads[h], kb, (((1,), (1,)), ((), ())),
                                preferred_element_type=F32)
            s = s + pltpu.repeat(f_q[h], reps, 1) - f_k
            if masked:
                s = jnp.where(causal, s, NEG_INF)
            m_old = m_ref[h]
            m_new = jnp.maximum(m_old, jnp.max(s, axis=-1, keepdims=True))
            alpha = jnp.exp(m_old - m_new)
            p = jnp.exp(s - pltpu.repeat(m_new, reps, 1))
            acc_ref[h] = alpha * acc_ref[h] + jnp.dot(p.astype(BF16), v_heads[h],
                                                      preferred_element_type=F32)
            m_ref[h] = m_new

    def body(j, _):
        step(j, False)
        return 0

    lax.fori_loop(0, qi, body, 0)
    step(qi, True)
    outs = [acc_ref[h] / pltpu.roll(acc_ref[h], HEAD_DIM, 1) for h in range(2)]
    o_ref[...] = jnp.where(first, outs[0], outs[1]).astype(o_ref.dtype)


def _forgetting_attention(qkv, f_cum, f_cum_t, bsz, seq, n_heads, tq=512):
    t = qkv.shape[0]
    n_pairs = n_heads // 2
    nq = seq // tq
    return pl.pallas_call(
        functools.partial(_fox_kernel, tq=tq),
        out_shape=jax.ShapeDtypeStruct((t, n_heads * HEAD_DIM), BF16),
        grid=(bsz, n_pairs, nq),
        in_specs=[pl.BlockSpec((tq, PAIR), lambda b, p, i: (b * nq + i, p)),
                  pl.BlockSpec((seq, PAIR), lambda b, p, i: (b, n_pairs + p)),
                  pl.BlockSpec((seq, PAIR), lambda b, p, i: (b, 2 * n_pairs + p)),
                  pl.BlockSpec((tq, LANES), lambda b, p, i: (b * nq + i, 0)),
                  pl.BlockSpec((None, n_heads, seq), lambda b, p, i: (b, 0, 0))],
        out_specs=pl.BlockSpec((tq, PAIR), lambda b, p, i: (b * nq + i, p)),
        scratch_shapes=[pltpu.VMEM((2, tq, PAIR), F32), pltpu.VMEM((2, tq, LANES), F32)],
        compiler_params=_cparams(("parallel", "parallel", "arbitrary")),
        name="forgetting_attention",
    )(qkv, qkv, qkv, f_cum, f_cum_t)


def _conv_silu_kernel(x_ref, halo_ref, w_ref, b_ref, o_ref, *, tb, blocks_per_seq, width):
    i = pl.program_id(0)
    x = x_ref[...]
    halo = jnp.where(i % blocks_per_seq == 0, jnp.zeros_like(halo_ref[...]), halo_ref[...])
    xx = jnp.concatenate([halo, x], axis=0)
    w = w_ref[...]
    y = b_ref[...] + w[width - 1:width, :] * x
    for tap in range(width - 1):
        shift = width - 1 - tap
        y = y + w[tap:tap + 1, :] * xx[8 - shift:8 - shift + tb, :]
    o_ref[...] = y * (1.0 / (1.0 + jnp.exp(-y)))


def _conv_silu(rest, conv_w, conv_b, seq, col0, n_ch, tb=512):
    t = rest.shape[0]
    width = conv_w.shape[0]
    tc = math.gcd(col0, n_ch)
    cb0 = col0 // tc
    hb = tb // 8
    return pl.pallas_call(
        functools.partial(_conv_silu_kernel, tb=tb, blocks_per_seq=seq // tb, width=width),
        out_shape=jax.ShapeDtypeStruct((t, n_ch), F32),
        grid=(t // tb, n_ch // tc),
        in_specs=[pl.BlockSpec((tb, tc), lambda i, j: (i, cb0 + j)),
                  pl.BlockSpec((8, tc), lambda i, j: (jnp.maximum(i * hb - 1, 0), cb0 + j)),
                  pl.BlockSpec((width, tc), lambda i, j: (0, j)),
                  pl.BlockSpec((1, tc), lambda i, j: (0, j))],
        out_specs=pl.BlockSpec((tb, tc), lambda i, j: (i, j)),
        compiler_params=_cparams(("parallel", "parallel")),
        name="causal_conv_silu",
    )(rest, rest, conv_w, conv_b.reshape(1, n_ch))


def _ssd_kernel(xc_ref, z_ref, dt_ref, dtt_ref, bias_row_ref, bias_col_ref, alog_row_ref,
                alog_col_ref, dskip_ref, gate_ref, o_ref, state_ref, *, n_heads, d_inner):
    q = SSD_CHUNK
    n = SSM_STATE
    hpg = n_heads // SSM_GROUPS
    gw = hpg * HEAD_DIM

    @pl.when(pl.program_id(1) == 0)
    def _():
        state_ref[...] = jnp.zeros_like(state_ref)

    row = lax.broadcasted_iota(jnp.int32, (q, q), 0)
    col = lax.broadcasted_iota(jnp.int32, (q, q), 1)
    causal = col <= row
    lower = jnp.where(causal, 1.0, 0.0).astype(BF16)
    upper = jnp.where(row <= col, 1.0, 0.0).astype(BF16)
    er = lax.broadcasted_iota(jnp.int32, (LANES, d_inner), 0)
    ec = lax.broadcasted_iota(jnp.int32, (LANES, d_inner), 1)
    expand = jnp.where(ec // HEAD_DIM == er, 1.0, 0.0).astype(BF16)
    first = _head_masks(q)

    dt = _softplus(dt_ref[...] + bias_row_ref[...])
    a_cum = _split_dot(lower, dt * (-jnp.exp(alog_row_ref[...])), 3)
    a_last = a_cum[q - 1:q, :]
    dt_t = _softplus(dtt_ref[...] + bias_col_ref[...])
    a_cum_t = _dot_split(dt_t * (-jnp.exp(alog_col_ref[...])), upper, 3)

    dt_x = _dot_split(dt, expand, 2)
    decay_in_x = _dot_split(jnp.exp(a_cum), expand, 2)
    decay_out_x = _dot_split(jnp.exp(a_last - a_cum), expand, 2)
    chunk_decay_x = _dot_split(jnp.broadcast_to(jnp.exp(a_last), (8, LANES)), expand, 2)[0:1, :]

    xs = xc_ref[:, 0:d_inner]
    xdt = xs * dt_x
    xdt_bf = xdt.astype(BF16)
    xend_bf = (xdt * decay_out_x).astype(BF16)

    y_groups = []
    for g in range(SSM_GROUPS):
        b_g = xc_ref[:, d_inner + g * n:d_inner + (g + 1) * n].astype(BF16)
        c_g = xc_ref[:, d_inner + (SSM_GROUPS + g) * n:d_inner + (SSM_GROUPS + g + 1) * n].astype(BF16)
        cb = lax.dot_general(c_g, b_g, (((1,), (1,)), ((), ())), preferred_element_type=F32)
        state = state_ref[g]
        y_off = jnp.dot(c_g, state.astype(BF16), preferred_element_type=F32)
        y_g = y_off * decay_in_x[:, g * gw:(g + 1) * gw]
        diag_pairs = []
        for pr in range(hpg // 2):
            lo = g * gw + pr * PAIR
            x_pair = xdt_bf[:, lo:lo + PAIR]
            acc = None
            for hh in range(2):
                h = g * hpg + pr * 2 + hh
                seg = a_cum[:, h:h + 1] - a_cum_t[h:h + 1, :]
                m = (cb * jnp.where(causal, jnp.exp(seg), 0.0)).astype(BF16)
                x_h = jnp.where(first, x_pair, jnp.zeros_like(x_pair)) if hh == 0 else \
                    jnp.where(first, jnp.zeros_like(x_pair), x_pair)
                d = jnp.dot(m, x_h, preferred_element_type=F32)
                acc = d if acc is None else acc + d
            diag_pairs.append(acc)
        y_groups.append(y_g + jnp.concatenate(diag_pairs, axis=1))
        new_state = lax.dot_general(b_g, xend_bf[:, g * gw:(g + 1) * gw], (((0,), (0,)), ((), ())),
                                    preferred_element_type=F32)
        state_ref[g] = state * chunk_decay_x[:, g * gw:(g + 1) * gw] + new_state

    y = jnp.concatenate(y_groups, axis=1) + dskip_ref[...] * xs
    z = z_ref[...]
    gated = y * (z * (1.0 / (1.0 + jnp.exp(-z))))
    gn = d_inner // SSM_GROUPS
    normed = []
    for g in range(SSM_GROUPS):
        gg = gated[:, g * gn:(g + 1) * gn]
        normed.append(gg * lax.rsqrt(jnp.mean(gg * gg, axis=-1, keepdims=True) + NORM_EPS))
    o_ref[...] = (jnp.concatenate(normed, axis=1) * gate_ref[...]).astype(o_ref.dtype)


def _ssd(conv_out, rest, dt_t, dt_bias, a_log, d_skip, gate_norm, bsz, seq, n_heads, d_inner,
         z_col, dt_col_block):
    t = conv_out.shape[0]
    nc = seq // SSD_CHUNK
    conv_ch = conv_out.shape[1]
    pad = LANES - n_heads
    row = lambda v: jnp.pad(v, (0, pad)).reshape(1, LANES)
    colv = lambda v: v.reshape(n_heads, 1)
    zb = z_col // d_inner
    const = lambda shape: pl.BlockSpec(shape, lambda b, c: (0,) * len(shape))
    return pl.pallas_call(
        functools.partial(_ssd_kernel, n_heads=n_heads, d_inner=d_inner),
        out_shape=jax.ShapeDtypeStruct((t, d_inner), BF16),
        grid=(bsz, nc),
        in_specs=[pl.BlockSpec((SSD_CHUNK, conv_ch), lambda b, c: (b * nc + c, 0)),
                  pl.BlockSpec((SSD_CHUNK, d_inner), lambda b, c: (b * nc + c, zb)),
                  pl.BlockSpec((SSD_CHUNK, LANES), lambda b, c: (b * nc + c, dt_col_block)),
                  pl.BlockSpec((None, n_heads, SSD_CHUNK), lambda b, c: (b, 0, c)),
                  const((1, LANES)), const((n_heads, 1)), const((1, LANES)), const((n_heads, 1)),
                  const((1, d_inner)), const((1, d_inner))],
        out_specs=pl.BlockSpec((SSD_CHUNK, d_inner), lambda b, c: (b * nc + c, 0)),
        scratch_shapes=[pltpu.VMEM((SSM_GROUPS, SSM_STATE, d_inner // SSM_GROUPS), F32)],
        compiler_params=_cparams(("parallel", "arbitrary")),
        name="ssd_chunk_scan",
    )(conv_out, rest, rest, dt_t, row(dt_bias), colv(dt_bias), row(a_log), colv(a_log),
      jnp.repeat(d_skip, HEAD_DIM).reshape(1, d_inner), gate_norm.reshape(1, d_inner))


def _even_mixer(h, w_in, w_out, post_w, resid, bsz, seq):
    d = h.shape[1]
    n_heads = d // (2 * HEAD_DIM)
    hw = n_heads * HEAD_DIM
    w_bf = w_in.astype(BF16)
    qkv_a = _matmul(h, w_bf[:, :3 * hw], BF16)
    qkv_b = _matmul(h, w_bf[:, 3 * hw:], F32)
    oa = _stick_breaking(qkv_a, bsz, seq, n_heads, 0, hw, 2 * hw)
    slopes = jnp.exp2(-ALIBI_MAX_EXP * jnp.arange(1, n_heads + 1, dtype=F32) / n_heads)
    ob = _dilated_attention(qkv_b, slopes, bsz, seq, n_heads)
    mixed = jnp.concatenate([oa, ob], axis=1)
    return _matmul_norm_residual(mixed, w_out.astype(BF16), post_w, resid)


def _odd_mixer(h, w_in, b_f, conv_w, conv_b, dt_bias, a_log, d_skip, gate_norm, w_out, post_w,
               resid, bsz, seq):
    d = h.shape[1]
    n_heads = d // (2 * HEAD_DIM)
    hw = n_heads * HEAD_DIM
    d_inner = hw
    conv_ch = d_inner + 2 * SSM_GROUPS * SSM_STATE
    pad = LANES - n_heads
    c0 = 3 * hw
    w_qkv = w_in[:, :c0].astype(BF16)
    f_w = w_in[:, c0:c0 + n_heads]
    z_w = w_in[:, c0 + n_heads:c0 + n_heads + d_inner]
    xbc_w = w_in[:, c0 + n_heads + d_inner:c0 + n_heads + d_inner + conv_ch]
    dt_w = w_in[:, c0 + n_heads + d_inner + conv_ch:]
    w_rest = jnp.concatenate([z_w, xbc_w, jnp.pad(f_w, ((0, 0), (0, pad))),
                              jnp.pad(dt_w, ((0, 0), (0, pad)))], axis=1).astype(BF16)
    qkv = _matmul(h, w_qkv, BF16)
    rest = _matmul(h, w_rest, F32)
    f_block = (d_inner + conv_ch) // LANES
    dt_block = f_block + 1

    f_cum = _logf_cumsum(rest, jnp.pad(b_f, (0, pad)).reshape(1, LANES), bsz, seq, f_block)
    f_cum_t = f_cum[:, :n_heads].reshape(bsz, seq, n_heads).transpose(0, 2, 1)
    oc = _forgetting_attention(qkv, f_cum, f_cum_t, bsz, seq, n_heads)

    conv_out = _conv_silu(rest, conv_w, conv_b, seq, d_inner, conv_ch)
    dt_col = dt_block * LANES
    dt_t = rest[:, dt_col:dt_col + n_heads].reshape(bsz, seq, n_heads).transpose(0, 2, 1)
    y = _ssd(conv_out, rest, dt_t, dt_bias, a_log, d_skip, gate_norm, bsz, seq, n_heads, d_inner,
             0, dt_block)
    mixed = jnp.concatenate([oc, y], axis=1)
    return _matmul_norm_residual(mixed, w_out.astype(BF16), post_w, resid)


def _mlp(h, w_up, w_down, post_w, resid):
    hidden = _matmul(h, w_up.astype(BF16), BF16, relu2=True)
    return _matmul_norm_residual(hidden, w_down.astype(BF16), post_w, resid)


def kernel(x, mix_norm_pre, mix_norm_post, mlp_norm_pre, mlp_norm_post, ab_w_in, ab_w_out,
           cd_w_in, cd_b_f, cd_conv_w, cd_conv_b, cd_dt_bias, cd_a_log, cd_d_skip,
           cd_gate_norm, cd_w_out, mlp_w_up, mlp_w_down):
    bsz, seq, d = x.shape
    depth = mix_norm_pre.shape[0]
    xf = x.reshape(bsz * seq, d)
    for layer in range(depth):
        i = layer // 2
        h = _rmsnorm(xf, mix_norm_pre[layer])
        if layer % 2 == 0:
            xf = _even_mixer(h, ab_w_in[i], ab_w_out[i], mix_norm_post[layer], xf, bsz, seq)
        else:
            xf = _odd_mixer(h, cd_w_in[i], cd_b_f[i], cd_conv_w[i], cd_conv_b[i], cd_dt_bias[i],
                            cd_a_log[i], cd_d_skip[i], cd_gate_norm[i], cd_w_out[i],
                            mix_norm_post[layer], xf, bsz, seq)
        h = _rmsnorm(xf, mlp_norm_pre[layer])
        xf = _mlp(h, mlp_w_up[layer], mlp_w_down[layer], mlp_norm_post[layer], xf)
    return xf.reshape(bsz, seq, d)
```

```python
import functools
import math

import jax
import jax.numpy as jnp
from jax import lax
from jax.experimental import pallas as pl
from jax.experimental.pallas import tpu as pltpu

F32 = jnp.float32
BF16 = jnp.bfloat16

LANES = 128
HEAD_DIM = 64
PAIR = 2 * HEAD_DIM
NORM_EPS = 1e-6
NEG_INF = -1e30
DILATED_PATTERNS = ((128, 1), (512, 4), (2048, 16))
ALIBI_MAX_EXP = 8.0
SSM_GROUPS = 4
SSM_STATE = 128
SSD_CHUNK = 128
SB_LOG_UNDERFLOW = -104.0
FOX_LOG_UNDERFLOW = 104.0
FOX_NORM_SLACK = 1.001
VMEM_LIMIT_BYTES = 56 * 1024 * 1024


def _cparams(sem):
    return pltpu.CompilerParams(dimension_semantics=sem, vmem_limit_bytes=VMEM_LIMIT_BYTES)


def _softplus(x):
    return jnp.maximum(x, 0.0) + jnp.log1p(jnp.exp(-jnp.abs(x)))


def _split_bf16(x, parts):
    out = []
    rem = x
    for _ in range(parts):
        hi = rem.astype(BF16)
        out.append(hi)
        rem = rem - hi.astype(F32)
    return out


def _dot_split(x, m, parts):
    acc = None
    for t in _split_bf16(x, parts):
        d = jnp.dot(t, m, preferred_element_type=F32)
        acc = d if acc is None else acc + d
    return acc


def _split_dot(m, x, parts):
    acc = None
    for t in _split_bf16(x, parts):
        d = jnp.dot(m, t, preferred_element_type=F32)
        acc = d if acc is None else acc + d
    return acc


def _rmsnorm_kernel(x_ref, w_ref, o_ref):
    x = x_ref[...]
    y = x * lax.rsqrt(jnp.mean(x * x, axis=-1, keepdims=True) + NORM_EPS)
    o_ref[...] = (y * w_ref[...]).astype(o_ref.dtype)


def _rmsnorm(x2d, w, tm=512):
    t, d = x2d.shape
    return pl.pallas_call(
        _rmsnorm_kernel,
        out_shape=jax.ShapeDtypeStruct((t, d), BF16),
        grid=(t // tm,),
        in_specs=[pl.BlockSpec((tm, d), lambda i: (i, 0)),
                  pl.BlockSpec((1, d), lambda i: (0, 0))],
        out_specs=pl.BlockSpec((tm, d), lambda i: (i, 0)),
        compiler_params=_cparams(("parallel",)),
        name="rmsnorm",
    )(x2d, w.reshape(1, d))


def _mm_kernel(a_ref, b_ref, o_ref, *scratch, nk, relu2):
    def finish(acc):
        if relu2:
            acc = jnp.square(jnp.maximum(acc, 0.0))
        o_ref[...] = acc.astype(o_ref.dtype)

    if nk == 1:
        finish(jnp.dot(a_ref[...], b_ref[...], preferred_element_type=F32))
        return
    acc_ref, = scratch
    k = pl.program_id(2)

    @pl.when(k == 0)
    def _():
        acc_ref[...] = jnp.zeros_like(acc_ref)

    acc_ref[...] += jnp.dot(a_ref[...], b_ref[...], preferred_element_type=F32)

    @pl.when(k == nk - 1)
    def _():
        finish(acc_ref[...])


def _matmul(a, b, out_dtype, *, relu2=False, tm=1024, tn=1792, tk=2048):
    m, kk = a.shape
    n = b.shape[1]
    tm, tn, tk = min(tm, m), min(tn, n), min(tk, kk)
    while n % tn:
        tn -= LANES
    nk = kk // tk
    scratch = [] if nk == 1 else [pltpu.VMEM((tm, tn), F32)]
    return pl.pallas_call(
        functools.partial(_mm_kernel, nk=nk, relu2=relu2),
        out_shape=jax.ShapeDtypeStruct((m, n), out_dtype),
        grid=(m // tm, n // tn, nk),
        in_specs=[pl.BlockSpec((tm, tk), lambda i, j, k: (i, k)),
                  pl.BlockSpec((tk, tn), lambda i, j, k: (k, j))],
        out_specs=pl.BlockSpec((tm, tn), lambda i, j, k: (i, j)),
        scratch_shapes=scratch,
        compiler_params=_cparams(("parallel", "parallel", "arbitrary")),
        name="matmul",
    )(a, b)


def _mm_norm_res_kernel(*refs, steps, n_chunk, m_chunk, with_next):
    n_parts = len(steps)
    a_refs = refs[:n_parts]
    b_ref, w_ref, r_ref = refs[n_parts:n_parts + 3]
    rest = refs[n_parts + 3:]
    wn_ref, o_ref, h_ref = rest if with_next else (None, rest[0], None)
    k = pl.program_id(1)
    nk = sum(steps)
    tm, n = o_ref.shape

    @pl.when(k == 0)
    def _():
        o_ref[...] = jnp.zeros_like(o_ref)

    def accumulate(a_ref):
        a = a_ref[...]
        for n0 in range(0, n, n_chunk):
            o_ref[:, n0:n0 + n_chunk] += jnp.dot(a, b_ref[:, n0:n0 + n_chunk],
                                                 preferred_element_type=F32)

    lo = 0
    for a_ref, cnt in zip(a_refs, steps):
        if n_parts == 1:
            accumulate(a_ref)
        else:
            pl.when(jnp.logical_and(k >= lo, k < lo + cnt))(functools.partial(accumulate, a_ref))
        lo += cnt

    @pl.when(k == nk - 1)
    def _():
        for m0 in range(0, tm, m_chunk):
            rows = slice(m0, m0 + m_chunk)
            y = o_ref[rows, :]
            y = y * lax.rsqrt(jnp.mean(y * y, axis=-1, keepdims=True) + NORM_EPS)
            x_new = r_ref[rows, :] + y * w_ref[...]
            o_ref[rows, :] = x_new
            if with_next:
                hn = x_new * lax.rsqrt(jnp.mean(x_new * x_new, axis=-1, keepdims=True) + NORM_EPS)
                h_ref[rows, :] = (hn * wn_ref[...]).astype(h_ref.dtype)


def _matmul_norm_residual(a_parts, b, w, resid, next_w=None, *, tm=1024, tk=512):
    m = a_parts[0].shape[0]
    n = b.shape[1]
    tm = min(tm, m)
    tk = min([tk] + [a.shape[1] for a in a_parts])
    steps = tuple(a.shape[1] // tk for a in a_parts)
    starts = [sum(steps[:p]) for p in range(len(steps))]
    with_next = next_w is not None

    def a_spec(start, cnt):
        return pl.BlockSpec((tm, tk), lambda i, k: (i, jnp.clip(k - start, 0, cnt - 1)))

    row_spec = pl.BlockSpec((1, n), lambda i, k: (0, 0))
    tile_spec = pl.BlockSpec((tm, n), lambda i, k: (i, 0))
    in_specs = [a_spec(s, c) for s, c in zip(starts, steps)]
    in_specs += [pl.BlockSpec((tk, n), lambda i, k: (k, 0)), row_spec, tile_spec]
    args = list(a_parts) + [b, w.reshape(1, n), resid]
    out_shape = jax.ShapeDtypeStruct((m, n), F32)
    out_specs = tile_spec
    if with_next:
        in_specs.append(row_spec)
        args.append(next_w.reshape(1, n))
        out_shape = (out_shape, jax.ShapeDtypeStruct((m, n), BF16))
        out_specs = (tile_spec, tile_spec)
    return pl.pallas_call(
        functools.partial(_mm_norm_res_kernel, steps=steps, n_chunk=min(n, 512),
                          m_chunk=min(tm, 256), with_next=with_next),
        out_shape=out_shape,
        grid=(m // tm, sum(steps)),
        in_specs=in_specs,
        out_specs=out_specs,
        compiler_params=_cparams(("parallel", "arbitrary")),
        name="matmul_norm_residual",
    )(*args)


def _head_masks(rows):
    lane = lax.broadcasted_iota(jnp.int32, (rows, PAIR), 1)
    return lane < HEAD_DIM


def _sb_kernel(q_ref, k_ref, v_ref, upper_ref, o_ref, acc_ref, c_ref, *, tq):
    qi = pl.program_id(2)
    span = 2 * tq
    first = _head_masks(tq)
    q = q_ref[...] * jnp.asarray(HEAD_DIM ** -0.5, BF16)
    q_heads = (jnp.where(first, q, jnp.zeros_like(q)), jnp.where(first, jnp.zeros_like(q), q))

    def scores(h, kb):
        z = lax.dot_general(q_heads[h], kb, (((1,), (1,)), ((), ())), preferred_element_type=F32)
        soft = jnp.log1p(jnp.exp(-jnp.abs(z)))
        return jnp.minimum(z, 0.0) - soft, -jnp.maximum(z, 0.0) - soft

    t0 = qi * tq
    first_block = jnp.maximum(qi - 1, 0)
    k0 = pl.multiple_of(first_block * tq, tq)
    row = lax.broadcasted_iota(jnp.int32, (tq, span), 0)
    col = lax.broadcasted_iota(jnp.int32, (tq, span), 1)
    strict = col < row + (t0 - k0)
    kb = k_ref[pl.ds(k0, span), :]
    vb = v_ref[pl.ds(k0, span), :]
    for h in range(2):
        log_beta, log_keep = scores(h, kb)
        log_keep = jnp.where(strict, log_keep, 0.0)
        later = jnp.dot(log_keep.astype(BF16), upper_ref[...], preferred_element_type=F32)
        w = jnp.where(strict, jnp.exp(log_beta + later), 0.0)
        acc_ref[h] = jnp.dot(w.astype(BF16), vb, preferred_element_type=F32)
        c_ref[h] = jnp.sum(log_keep, axis=1, keepdims=True)

    def cond(carry):
        j, live = carry
        return jnp.logical_and(j >= 0, live)

    def body(carry):
        j, _ = carry
        kj = pl.multiple_of(j * tq, tq)
        kb = k_ref[pl.ds(kj, tq), :]
        vb = v_ref[pl.ds(kj, tq), :]
        for h in range(2):
            log_beta, log_keep = scores(h, kb)
            later = jnp.dot(log_keep.astype(BF16), upper_ref[0:tq, 0:tq],
                            preferred_element_type=F32)
            c = c_ref[h]
            w = jnp.exp(log_beta + later + c)
            acc_ref[h] += jnp.dot(w.astype(BF16), vb, preferred_element_type=F32)
            c_ref[h] = c + jnp.sum(log_keep, axis=1, keepdims=True)
        return j - 1, jnp.max(c_ref[...]) > SB_LOG_UNDERFLOW

    lax.while_loop(cond, body, (first_block - 1, jnp.max(c_ref[...]) > SB_LOG_UNDERFLOW))
    o_ref[...] = jnp.where(first, acc_ref[0], acc_ref[1]).astype(o_ref.dtype)


def _stick_breaking(proj, bsz, seq, n_heads, q_col, k_col, v_col, tq=256):
    t = proj.shape[0]
    n_pairs = n_heads // 2
    qb, kb, vb = q_col // PAIR, k_col // PAIR, v_col // PAIR
    nq = seq // tq
    span = 2 * tq
    upper = (jnp.arange(span)[:, None] > jnp.arange(span)[None, :]).astype(BF16)
    return pl.pallas_call(
        functools.partial(_sb_kernel, tq=tq),
        out_shape=jax.ShapeDtypeStruct((t, n_heads * HEAD_DIM), BF16),
        grid=(bsz, n_pairs, nq),
        in_specs=[pl.BlockSpec((tq, PAIR), lambda b, p, i: (b * nq + i, qb + p)),
                  pl.BlockSpec((seq, PAIR), lambda b, p, i: (b, kb + p)),
                  pl.BlockSpec((seq, PAIR), lambda b, p, i: (b, vb + p)),
                  pl.BlockSpec((span, span), lambda b, p, i: (0, 0))],
        out_specs=pl.BlockSpec((tq, PAIR), lambda b, p, i: (b * nq + i, p)),
        scratch_shapes=[pltpu.VMEM((2, tq, PAIR), F32), pltpu.VMEM((2, tq, 1), F32)],
        compiler_params=_cparams(("parallel", "parallel", "arbitrary")),
        name="stick_breaking_attention",
    )(proj, proj, proj, upper)


def _dilated_kernel(slope_ref, q_ref, k_ref, v_ref, o_ref, num_ref, den_ref, max_ref, *, seq, tq):
    pair = pl.program_id(1)
    first = _head_masks(tq)
    span = 2 * tq
    first_keys = _head_masks(span)
    ones = jnp.ones((span, PAIR), BF16)
    row = lax.broadcasted_iota(jnp.int32, (tq, span), 0)
    col = lax.broadcasted_iota(jnp.int32, (tq, span), 1)
    scale = HEAD_DIM ** -0.5

    for branch, (window, r) in enumerate(DILATED_PATTERNS):
        assert window // r == tq
        per_class = seq // (r * tq)
        stride = None if r == 1 else r

        def bias(offset, h):
            hops = offset + row - col
            valid = jnp.logical_and(hops >= 0, hops <= tq)
            return jnp.where(valid, -(slope_ref[2 * pair + h] * (hops * r).astype(F32)), NEG_INF)

        bias_first = [bias(0, h) for h in range(2)]
        bias_rest = [bias(tq, h) for h in range(2)]

        def body(n, _):
            cls = n // per_class
            i = n - cls * per_class
            m0 = i * tq
            k0 = jnp.maximum(m0 - tq, 0)
            q_rows = pl.ds(cls + r * m0, tq, stride=stride)
            k_rows = pl.ds(cls + r * k0, span, stride=stride)
            q = (q_ref[q_rows, :] * scale).astype(BF16)
            kc = k_ref[k_rows, :].astype(BF16)
            vc = v_ref[k_rows, :].astype(BF16)
            q_heads = (jnp.where(first, q, jnp.zeros_like(q)), jnp.where(first, jnp.zeros_like(q), q))
            v_heads = (jnp.where(first_keys, vc, ones), jnp.where(first_keys, ones, vc))
            res, maxes = [], []
            for h in range(2):
                z = lax.dot_general(q_heads[h], kc, (((1,), (1,)), ((), ())),
                                    preferred_element_type=F32)
                logits = z + jnp.where(i == 0, bias_first[h], bias_rest[h])
                m = jnp.max(logits, axis=-1, keepdims=True)
                p = jnp.exp(logits - m)
                res.append(jnp.dot(p.astype(BF16), v_heads[h], preferred_element_type=F32))
                maxes.append(jnp.broadcast_to(m, (tq, PAIR)))
            num = jnp.where(first, res[0], res[1])
            den = pltpu.roll(jnp.where(first, res[1], res[0]), HEAD_DIM, 1)
            mx = jnp.where(first, maxes[0], maxes[1])
            if branch == 0:
                num_ref[q_rows, :] = num
                den_ref[q_rows, :] = den
                max_ref[q_rows, :] = mx
            else:
                m_old = max_ref[q_rows, :]
                m_new = jnp.maximum(m_old, mx)
                w_old = jnp.exp(m_old - m_new)
                w_new = jnp.exp(mx - m_new)
                num_ref[q_rows, :] = w_old * num_ref[q_rows, :] + w_new * num
                den_ref[q_rows, :] = w_old * den_ref[q_rows, :] + w_new * den
                max_ref[q_rows, :] = m_new
            return 0

        lax.fori_loop(0, seq // tq, body, 0, unroll=8)

    o_ref[...] = (num_ref[...] / den_ref[...]).astype(o_ref.dtype)


def _dilated_attention(qkv, slopes, bsz, seq, n_heads):
    t = qkv.shape[0]
    n_pairs = n_heads // 2
    tq = DILATED_PATTERNS[0][0] // DILATED_PATTERNS[0][1]
    spec = lambda off: pl.BlockSpec((seq, PAIR), lambda b, p: (b, off + p))
    return pl.pallas_call(
        functools.partial(_dilated_kernel, seq=seq, tq=tq),
        out_shape=jax.ShapeDtypeStruct((t, n_heads * HEAD_DIM), BF16),
        grid=(bsz, n_pairs),
        in_specs=[pl.BlockSpec(memory_space=pltpu.SMEM), spec(0), spec(n_pairs), spec(2 * n_pairs)],
        out_specs=pl.BlockSpec((seq, PAIR), lambda b, p: (b, p)),
        scratch_shapes=[pltpu.VMEM((seq, PAIR), F32)] * 3,
        compiler_params=_cparams(("parallel", "parallel")),
        name="dilated_window_attention",
    )(slopes, qkv, qkv, qkv)


def _logf_cumsum_kernel(f_ref, b_ref, o_ref, floor_ref, carry_ref, low_ref, *, tb):
    i = pl.program_id(1)

    @pl.when(i == 0)
    def _():
        carry_ref[...] = jnp.zeros_like(carry_ref)
        low_ref[...] = jnp.zeros_like(low_ref)

    x = f_ref[...] + b_ref[...]
    log_f = jnp.minimum(x, 0.0) - jnp.log1p(jnp.exp(-jnp.abs(x)))
    row = lax.broadcasted_iota(jnp.int32, (tb, tb), 0)
    col = lax.broadcasted_iota(jnp.int32, (tb, tb), 1)
    lower = jnp.where(col <= row, 1.0, 0.0).astype(BF16)
    cum = _split_dot(lower, log_f, 3) + carry_ref[...]
    o_ref[...] = cum
    carry_ref[...] = cum[tb - 1:tb, :]
    low = jnp.minimum(low_ref[...], jnp.min(cum, axis=0, keepdims=True))
    low_ref[...] = low
    floor_ref[pl.ds(i, 1), :] = low


def _logf_cumsum(rest, b_f_row, bsz, seq, col_block, tb):
    t = rest.shape[0]
    nb = seq // tb
    return pl.pallas_call(
        functools.partial(_logf_cumsum_kernel, tb=tb),
        out_shape=(jax.ShapeDtypeStruct((t, LANES), F32),
                   jax.ShapeDtypeStruct((bsz, nb, LANES), F32)),
        grid=(bsz, nb),
        in_specs=[pl.BlockSpec((tb, LANES), lambda b, i: (b * nb + i, col_block)),
                  pl.BlockSpec((1, LANES), lambda b, i: (0, 0))],
        out_specs=(pl.BlockSpec((tb, LANES), lambda b, i: (b * nb + i, 0)),
                   pl.BlockSpec((None, nb, LANES), lambda b, i: (b, 0, 0))),
        scratch_shapes=[pltpu.VMEM((1, LANES), F32), pltpu.VMEM((1, LANES), F32)],
        compiler_params=_cparams(("parallel", "arbitrary")),
        name="log_forget_cumsum",
    )(rest, b_f_row)


def _fox_kernel(q_ref, k_ref, v_ref, fq_ref, fk_ref, floor_ref, o_ref, acc_ref, m_ref, knorm_ref,
                *, tq, seq):
    pair = pl.program_id(1)
    qi = pl.program_id(2)
    reps = tq // LANES
    first = _head_masks(tq)

    @pl.when(qi == 0)
    def _():
        def norms(c, carry):
            kk = k_ref[pl.ds(pl.multiple_of(c * tq, tq), tq), :].astype(F32)
            sq = kk * kk
            n0 = jnp.sum(jnp.where(first, sq, 0.0), axis=1, keepdims=True)
            n1 = jnp.sum(jnp.where(first, 0.0, sq), axis=1, keepdims=True)
            return jnp.maximum(carry[0], n0), jnp.maximum(carry[1], n1)

        zero = jnp.zeros((tq, 1), F32)
        n0, n1 = lax.fori_loop(0, seq // tq, norms, (zero, zero))
        knorm_ref[0] = jnp.sqrt(jnp.max(n0))
        knorm_ref[1] = jnp.sqrt(jnp.max(n1))

    q = q_ref[...] * jnp.asarray(HEAD_DIM ** -0.5, BF16)
    q_heads = (jnp.where(first, q, jnp.zeros_like(q)), jnp.where(first, jnp.zeros_like(q), q))
    fq_all = fq_ref[...]
    lane = lax.broadcasted_iota(jnp.int32, (tq, LANES), 1)
    f_q = [jnp.broadcast_to(jnp.sum(jnp.where(lane == 2 * pair + h, fq_all, 0.0), axis=1,
                                    keepdims=True), (tq, LANES)) for h in range(2)]
    row = lax.broadcasted_iota(jnp.int32, (tq, tq), 0)
    col = lax.broadcasted_iota(jnp.int32, (tq, tq), 1)
    causal = col <= row
    ones = jnp.ones((tq, PAIR), BF16)

    acc_ref[...] = jnp.zeros_like(acc_ref)
    m_ref[...] = jnp.full_like(m_ref, NEG_INF)

    def step(j, masked):
        k0 = pl.multiple_of(j * tq, tq)
        kb = k_ref[pl.ds(k0, tq), :]
        vb = v_ref[pl.ds(k0, tq), :]
        v_heads = (jnp.where(first, vb, ones), jnp.where(first, ones, vb))
        for h in range(2):
            f_k = fk_ref[pl.ds(2 * pair + h, 1), pl.ds(k0, tq)]
            s = lax.dot_general(q_heads[h], kb, (((1,), (1,)), ((), ())),
                                preferred_element_type=F32)
            s = s + jnp.tile(f_q[h], (1, reps)) - f_k
            if masked:
                s = jnp.where(causal, s, NEG_INF)
            m_old = m_ref[h]
            m_new = jnp.maximum(m_old, jnp.max(s, axis=-1, keepdims=True))
            alpha = jnp.exp(m_old - m_new)
            p = jnp.exp(s - jnp.tile(m_new, (1, reps)))
            acc_ref[h] = alpha * acc_ref[h] + jnp.dot(p.astype(BF16), v_heads[h],
                                                      preferred_element_type=F32)
            m_ref[h] = m_new

    step(qi, True)

    floors = floor_ref[...]
    blk = lax.broadcasted_iota(jnp.int32, floors.shape, 0)
    head_lane = lax.broadcasted_iota(jnp.int32, floors.shape, 1)
    skip = []
    for h in range(2):
        q32 = q_heads[h].astype(F32)
        q_norm = jnp.sqrt(jnp.sum(q32 * q32, axis=1, keepdims=True))
        reach = f_q[h] + q_norm * (knorm_ref[h] * FOX_NORM_SLACK) - m_ref[h]
        limit = jnp.max(reach) + FOX_LOG_UNDERFLOW
        dead = jnp.logical_and(jnp.logical_and(head_lane == 2 * pair + h, blk < qi), floors > limit)
        skip.append(jnp.sum(jnp.where(dead, 1, 0)))
    start = jnp.minimum(skip[0], skip[1])

    def body(j, _):
        step(j, False)
        return 0

    lax.fori_loop(start, qi, body, 0)
    outs = [acc_ref[h] / pltpu.roll(acc_ref[h], HEAD_DIM, 1) for h in range(2)]
    o_ref[...] = jnp.where(first, outs[0], outs[1]).astype(o_ref.dtype)


FOX_BLOCK = 512


def _forgetting_attention(qkv, f_cum, f_cum_t, f_floor, bsz, seq, n_heads):
    t = qkv.shape[0]
    tq = FOX_BLOCK
    n_pairs = n_heads // 2
    nq = seq // tq
    return pl.pallas_call(
        functools.partial(_fox_kernel, tq=tq, seq=seq),
        out_shape=jax.ShapeDtypeStruct((t, n_heads * HEAD_DIM), BF16),
        grid=(bsz, n_pairs, nq),
        in_specs=[pl.BlockSpec((tq, PAIR), lambda b, p, i: (b * nq + i, p)),
                  pl.BlockSpec((seq, PAIR), lambda b, p, i: (b, n_pairs + p)),
                  pl.BlockSpec((seq, PAIR), lambda b, p, i: (b, 2 * n_pairs + p)),
                  pl.BlockSpec((tq, LANES), lambda b, p, i: (b * nq + i, 0)),
                  pl.BlockSpec((None, n_heads, seq), lambda b, p, i: (b, 0, 0)),
                  pl.BlockSpec((None, nq, LANES), lambda b, p, i: (b, 0, 0))],
        out_specs=pl.BlockSpec((tq, PAIR), lambda b, p, i: (b * nq + i, p)),
        scratch_shapes=[pltpu.VMEM((2, tq, PAIR), F32), pltpu.VMEM((2, tq, LANES), F32),
                        pltpu.SMEM((2,), F32)],
        compiler_params=_cparams(("arbitrary", "arbitrary", "arbitrary")),
        name="forgetting_attention",
    )(qkv, qkv, qkv, f_cum, f_cum_t, f_floor)


def _conv_silu_kernel(x_ref, halo_ref, w_ref, b_ref, o_ref, *, tb, blocks_per_seq, width):
    i = pl.program_id(0)
    x = x_ref[...]
    halo = jnp.where(i % blocks_per_seq == 0, jnp.zeros_like(halo_ref[...]), halo_ref[...])
    xx = jnp.concatenate([halo, x], axis=0)
    w = w_ref[...]
    y = b_ref[...] + w[width - 1:width, :] * x
    for tap in range(width - 1):
        shift = width - 1 - tap
        y = y + w[tap:tap + 1, :] * xx[8 - shift:8 - shift + tb, :]
    o_ref[...] = y * (1.0 / (1.0 + jnp.exp(-y)))


def _conv_silu(rest, conv_w, conv_b, seq, col0, n_ch, tb=512):
    t = rest.shape[0]
    width = conv_w.shape[0]
    tc = math.gcd(col0, n_ch)
    cb0 = col0 // tc
    hb = tb // 8
    return pl.pallas_call(
        functools.partial(_conv_silu_kernel, tb=tb, blocks_per_seq=seq // tb, width=width),
        out_shape=jax.ShapeDtypeStruct((t, n_ch), F32),
        grid=(t // tb, n_ch // tc),
        in_specs=[pl.BlockSpec((tb, tc), lambda i, j: (i, cb0 + j)),
                  pl.BlockSpec((8, tc), lambda i, j: (jnp.maximum(i * hb - 1, 0), cb0 + j)),
                  pl.BlockSpec((width, tc), lambda i, j: (0, j)),
                  pl.BlockSpec((1, tc), lambda i, j: (0, j))],
        out_specs=pl.BlockSpec((tb, tc), lambda i, j: (i, j)),
        compiler_params=_cparams(("parallel", "parallel")),
        name="causal_conv_silu",
    )(rest, rest, conv_w, conv_b.reshape(1, n_ch))


def _ssd_kernel(xc_ref, z_ref, dt_ref, dtt_ref, bias_row_ref, bias_col_ref, alog_row_ref,
                alog_col_ref, dskip_ref, gate_ref, o_ref, state_ref, *, n_heads, d_inner):
    q = SSD_CHUNK
    n = SSM_STATE
    hpg = n_heads // SSM_GROUPS
    gw = hpg * HEAD_DIM

    @pl.when(pl.program_id(1) == 0)
    def _():
        state_ref[...] = jnp.zeros_like(state_ref)

    row = lax.broadcasted_iota(jnp.int32, (q, q), 0)
    col = lax.broadcasted_iota(jnp.int32, (q, q), 1)
    causal = col <= row
    lower = jnp.where(causal, 1.0, 0.0).astype(BF16)
    upper = jnp.where(row <= col, 1.0, 0.0).astype(BF16)
    er = lax.broadcasted_iota(jnp.int32, (LANES, d_inner), 0)
    ec = lax.broadcasted_iota(jnp.int32, (LANES, d_inner), 1)
    expand = jnp.where(ec // HEAD_DIM == er, 1.0, 0.0).astype(BF16)
    first = _head_masks(q)

    dt = _softplus(dt_ref[...] + bias_row_ref[...])
    a_cum = _split_dot(lower, dt * (-jnp.exp(alog_row_ref[...])), 3)
    a_last = a_cum[q - 1:q, :]
    dt_t = _softplus(dtt_ref[...] + bias_col_ref[...])
    a_cum_t = _dot_split(dt_t * (-jnp.exp(alog_col_ref[...])), upper, 3)

    dt_x = _dot_split(dt, expand, 2)
    decay_in_x = _dot_split(jnp.exp(a_cum), expand, 2)
    decay_out_x = _dot_split(jnp.exp(a_last - a_cum), expand, 2)
    chunk_decay_x = _dot_split(jnp.broadcast_to(jnp.exp(a_last), (8, LANES)), expand, 2)[0:1, :]

    xs = xc_ref[:, 0:d_inner]
    xdt = xs * dt_x
    xdt_bf = xdt.astype(BF16)
    xend_bf = (xdt * decay_out_x).astype(BF16)

    y_groups = []
    for g in range(SSM_GROUPS):
        b_g = xc_ref[:, d_inner + g * n:d_inner + (g + 1) * n].astype(BF16)
        c_g = xc_ref[:, d_inner + (SSM_GROUPS + g) * n:d_inner + (SSM_GROUPS + g + 1) * n].astype(BF16)
        cb = lax.dot_general(c_g, b_g, (((1,), (1,)), ((), ())), preferred_element_type=F32)
        state = state_ref[g]
        y_off = jnp.dot(c_g, state.astype(BF16), preferred_element_type=F32)
        y_g = y_off * decay_in_x[:, g * gw:(g + 1) * gw]
        diag_pairs = []
        for pr in range(hpg // 2):
            lo = g * gw + pr * PAIR
            x_pair = xdt_bf[:, lo:lo + PAIR]
            acc = None
            for hh in range(2):
                h = g * hpg + pr * 2 + hh
                seg = a_cum[:, h:h + 1] - a_cum_t[h:h + 1, :]
                m = (cb * jnp.where(causal, jnp.exp(seg), 0.0)).astype(BF16)
                x_h = jnp.where(first, x_pair, jnp.zeros_like(x_pair)) if hh == 0 else \
                    jnp.where(first, jnp.zeros_like(x_pair), x_pair)
                d = jnp.dot(m, x_h, preferred_element_type=F32)
                acc = d if acc is None else acc + d
            diag_pairs.append(acc)
        y_groups.append(y_g + jnp.concatenate(diag_pairs, axis=1))
        new_state = lax.dot_general(b_g, xend_bf[:, g * gw:(g + 1) * gw], (((0,), (0,)), ((), ())),
                                    preferred_element_type=F32)
        state_ref[g] = state * chunk_decay_x[:, g * gw:(g + 1) * gw] + new_state

    y = jnp.concatenate(y_groups, axis=1) + dskip_ref[...] * xs
    z = z_ref[...]
    gated = y * (z * (1.0 / (1.0 + jnp.exp(-z))))
    gn = d_inner // SSM_GROUPS
    normed = []
    for g in range(SSM_GROUPS):
        gg = gated[:, g * gn:(g + 1) * gn]
        normed.append(gg * lax.rsqrt(jnp.mean(gg * gg, axis=-1, keepdims=True) + NORM_EPS))
    o_ref[...] = (jnp.concatenate(normed, axis=1) * gate_ref[...]).astype(o_ref.dtype)


def _ssd(conv_out, rest, dt_t, dt_bias, a_log, d_skip, gate_norm, bsz, seq, n_heads, d_inner,
         z_col, dt_col_block):
    t = conv_out.shape[0]
    nc = seq // SSD_CHUNK
    conv_ch = conv_out.shape[1]
    pad = LANES - n_heads
    row = lambda v: jnp.pad(v, (0, pad)).reshape(1, LANES)
    colv = lambda v: v.reshape(n_heads, 1)
    zb = z_col // d_inner
    const = lambda shape: pl.BlockSpec(shape, lambda b, c: (0,) * len(shape))
    return pl.pallas_call(
        functools.partial(_ssd_kernel, n_heads=n_heads, d_inner=d_inner),
        out_shape=jax.ShapeDtypeStruct((t, d_inner), BF16),
        grid=(bsz, nc),
        in_specs=[pl.BlockSpec((SSD_CHUNK, conv_ch), lambda b, c: (b * nc + c, 0)),
                  pl.BlockSpec((SSD_CHUNK, d_inner), lambda b, c: (b * nc + c, zb)),
                  pl.BlockSpec((SSD_CHUNK, LANES), lambda b, c: (b * nc + c, dt_col_block)),
                  pl.BlockSpec((None, n_heads, SSD_CHUNK), lambda b, c: (b, 0, c)),
                  const((1, LANES)), const((n_heads, 1)), const((1, LANES)), const((n_heads, 1)),
                  const((1, d_inner)), const((1, d_inner))],
        out_specs=pl.BlockSpec((SSD_CHUNK, d_inner), lambda b, c: (b * nc + c, 0)),
        scratch_shapes=[pltpu.VMEM((SSM_GROUPS, SSM_STATE, d_inner // SSM_GROUPS), F32)],
        compiler_params=_cparams(("parallel", "arbitrary")),
        name="ssd_chunk_scan",
    )(conv_out, rest, rest, dt_t, row(dt_bias), colv(dt_bias), row(a_log), colv(a_log),
      jnp.repeat(d_skip, HEAD_DIM).reshape(1, d_inner), gate_norm.reshape(1, d_inner))


def _even_mixer(h, w_in, w_out, post_w, resid, next_w, bsz, seq):
    d = h.shape[1]
    n_heads = d // (2 * HEAD_DIM)
    hw = n_heads * HEAD_DIM
    w_bf = w_in.astype(BF16)
    qkv_a = _matmul(h, w_bf[:, :3 * hw], BF16)
    qkv_b = _matmul(h, w_bf[:, 3 * hw:], F32)
    oa = _stick_breaking(qkv_a, bsz, seq, n_heads, 0, hw, 2 * hw)
    slopes = jnp.exp2(-ALIBI_MAX_EXP * jnp.arange(1, n_heads + 1, dtype=F32) / n_heads)
    ob = _dilated_attention(qkv_b, slopes, bsz, seq, n_heads)
    return _matmul_norm_residual([oa, ob], w_out.astype(BF16), post_w, resid, next_w)


def _odd_mixer(h, w_in, b_f, conv_w, conv_b, dt_bias, a_log, d_skip, gate_norm, w_out, post_w,
               resid, next_w, bsz, seq):
    d = h.shape[1]
    n_heads = d // (2 * HEAD_DIM)
    hw = n_heads * HEAD_DIM
    d_inner = hw
    conv_ch = d_inner + 2 * SSM_GROUPS * SSM_STATE
    pad = LANES - n_heads
    c0 = 3 * hw
    w_qkv = w_in[:, :c0].astype(BF16)
    f_w = w_in[:, c0:c0 + n_heads]
    z_w = w_in[:, c0 + n_heads:c0 + n_heads + d_inner]
    xbc_w = w_in[:, c0 + n_heads + d_inner:c0 + n_heads + d_inner + conv_ch]
    dt_w = w_in[:, c0 + n_heads + d_inner + conv_ch:]
    w_rest = jnp.concatenate([z_w, xbc_w, jnp.pad(f_w, ((0, 0), (0, pad))),
                              jnp.pad(dt_w, ((0, 0), (0, pad)))], axis=1).astype(BF16)
    qkv = _matmul(h, w_qkv, BF16)
    rest = _matmul(h, w_rest, F32)
    f_block = (d_inner + conv_ch) // LANES
    dt_block = f_block + 1

    f_cum, f_floor = _logf_cumsum(rest, jnp.pad(b_f, (0, pad)).reshape(1, LANES), bsz, seq, f_block,
                                  FOX_BLOCK)
    f_cum_t = f_cum[:, :n_heads].reshape(bsz, seq, n_heads).transpose(0, 2, 1)
    oc = _forgetting_attention(qkv, f_cum, f_cum_t, f_floor, bsz, seq, n_heads)

    conv_out = _conv_silu(rest, conv_w, conv_b, seq, d_inner, conv_ch)
    dt_col = dt_block * LANES
    dt_t = rest[:, dt_col:dt_col + n_heads].reshape(bsz, seq, n_heads).transpose(0, 2, 1)
    y = _ssd(conv_out, rest, dt_t, dt_bias, a_log, d_skip, gate_norm, bsz, seq, n_heads, d_inner,
             0, dt_block)
    return _matmul_norm_residual([oc, y], w_out.astype(BF16), post_w, resid, next_w)


def _mlp(h, w_up, w_down, post_w, resid, next_w):
    hidden = _matmul(h, w_up.astype(BF16), BF16, relu2=True)
    return _matmul_norm_residual([hidden], w_down.astype(BF16), post_w, resid, next_w)


def kernel(x, mix_norm_pre, mix_norm_post, mlp_norm_pre, mlp_norm_post, ab_w_in, ab_w_out,
           cd_w_in, cd_b_f, cd_conv_w, cd_conv_b, cd_dt_bias, cd_a_log, cd_d_skip,
           cd_gate_norm, cd_w_out, mlp_w_up, mlp_w_down):
    bsz, seq, d = x.shape
    depth = mix_norm_pre.shape[0]
    xf = x.reshape(bsz * seq, d)
    h = _rmsnorm(xf, mix_norm_pre[0])
    for layer in range(depth):
        i = layer // 2
        if layer % 2 == 0:
            xf, h = _even_mixer(h, ab_w_in[i], ab_w_out[i], mix_norm_post[layer], xf,
                                mlp_norm_pre[layer], bsz, seq)
        else:
            xf, h = _odd_mixer(h, cd_w_in[i], cd_b_f[i], cd_conv_w[i], cd_conv_b[i], cd_dt_bias[i],
                               cd_a_log[i], cd_d_skip[i], cd_gate_norm[i], cd_w_out[i],
                               mix_norm_post[layer], xf, mlp_norm_pre[layer], bsz, seq)
        if layer + 1 < depth:
            xf, h = _mlp(h, mlp_w_up[layer], mlp_w_down[layer], mlp_norm_post[layer], xf,
                         mix_norm_pre[layer + 1])
        else:
            xf = _mlp(h, mlp_w_up[layer], mlp_w_down[layer], mlp_norm_post[layer], xf, None)
    return xf.reshape(bsz, seq, d)
```

```python
import functools
import math

import jax
import jax.numpy as jnp
from jax import lax
from jax.experimental import pallas as pl
from jax.experimental.pallas import tpu as pltpu

F32 = jnp.float32
BF16 = jnp.bfloat16

LANES = 128
HEAD_DIM = 64
PAIR = 2 * HEAD_DIM
NORM_EPS = 1e-6
NEG_INF = -1e30
DILATED_PATTERNS = ((128, 1), (512, 4), (2048, 16))
DILATED_GROUP = 4
ALIBI_MAX_EXP = 8.0
SSM_GROUPS = 4
SSM_STATE = 128
SSD_CHUNK = 128
SB_LOG_UNDERFLOW = -104.0
FOX_LOG_UNDERFLOW = 104.0
FOX_NORM_SLACK = 1.001
VMEM_LIMIT_BYTES = 56 * 1024 * 1024


def _cparams(sem):
    return pltpu.CompilerParams(dimension_semantics=sem, vmem_limit_bytes=VMEM_LIMIT_BYTES)


def _softplus(x):
    return jnp.maximum(x, 0.0) + jnp.log1p(jnp.exp(-jnp.abs(x)))


def _split_bf16(x, parts):
    out = []
    rem = x
    for _ in range(parts):
        hi = rem.astype(BF16)
        out.append(hi)
        rem = rem - hi.astype(F32)
    return out


def _dot_split(x, m, parts):
    acc = None
    for t in _split_bf16(x, parts):
        d = jnp.dot(t, m, preferred_element_type=F32)
        acc = d if acc is None else acc + d
    return acc


def _split_dot(m, x, parts):
    acc = None
    for t in _split_bf16(x, parts):
        d = jnp.dot(m, t, preferred_element_type=F32)
        acc = d if acc is None else acc + d
    return acc


def _rmsnorm_kernel(x_ref, w_ref, o_ref):
    x = x_ref[...]
    y = x * lax.rsqrt(jnp.mean(x * x, axis=-1, keepdims=True) + NORM_EPS)
    o_ref[...] = (y * w_ref[...]).astype(o_ref.dtype)


def _rmsnorm(x2d, w, tm=512):
    t, d = x2d.shape
    return pl.pallas_call(
        _rmsnorm_kernel,
        out_shape=jax.ShapeDtypeStruct((t, d), BF16),
        grid=(t // tm,),
        in_specs=[pl.BlockSpec((tm, d), lambda i: (i, 0)),
                  pl.BlockSpec((1, d), lambda i: (0, 0))],
        out_specs=pl.BlockSpec((tm, d), lambda i: (i, 0)),
        compiler_params=_cparams(("parallel",)),
        name="rmsnorm",
    )(x2d, w.reshape(1, d))


def _mm_kernel(a_ref, b_ref, o_ref, *, relu2):
    acc = jnp.dot(a_ref[...], b_ref[...], preferred_element_type=F32)
    if relu2:
        acc = jnp.square(jnp.maximum(acc, 0.0))
    o_ref[...] = acc.astype(o_ref.dtype)


def _matmul(a, b, out_dtype, *, relu2=False, tm=1024, tn=1792):
    m, kk = a.shape
    n = b.shape[1]
    tm, tn = min(tm, m), min(tn, n)
    while n % tn:
        tn -= LANES
    return pl.pallas_call(
        functools.partial(_mm_kernel, relu2=relu2),
        out_shape=jax.ShapeDtypeStruct((m, n), out_dtype),
        grid=(m // tm, n // tn),
        in_specs=[pl.BlockSpec((tm, kk), lambda i, j: (i, 0)),
                  pl.BlockSpec((kk, tn), lambda i, j: (0, j))],
        out_specs=pl.BlockSpec((tm, tn), lambda i, j: (i, j)),
        compiler_params=_cparams(("parallel", "parallel")),
        name="matmul",
    )(a, b)


def _mm_f32w_kernel(a_ref, b_ref, o_ref, w_ref, *, relu2):
    @pl.when(pl.program_id(1) == 0)
    def _():
        w_ref[...] = b_ref[...].astype(BF16)

    acc = jnp.dot(a_ref[...], w_ref[...], preferred_element_type=F32)
    if relu2:
        acc = jnp.square(jnp.maximum(acc, 0.0))
    o_ref[...] = acc.astype(o_ref.dtype)


def _matmul_f32_weight(a, w, col0, n, out_dtype, *, relu2=False, tm=2048, tn=1024):
    m, kk = a.shape
    tm = min(tm, m)
    g = math.gcd(n, col0)
    tn = max(t for t in range(LANES, min(tn, g) + 1, LANES) if g % t == 0)
    jb = col0 // tn
    return pl.pallas_call(
        functools.partial(_mm_f32w_kernel, relu2=relu2),
        out_shape=jax.ShapeDtypeStruct((m, n), out_dtype),
        grid=(n // tn, m // tm),
        in_specs=[pl.BlockSpec((tm, kk), lambda j, i: (i, 0)),
                  pl.BlockSpec((kk, tn), lambda j, i: (0, jb + j))],
        out_specs=pl.BlockSpec((tm, tn), lambda j, i: (i, j)),
        scratch_shapes=[pltpu.VMEM((kk, tn), BF16)],
        compiler_params=_cparams(("arbitrary", "arbitrary")),
        name="matmul_f32_weight",
    )(a, w)


def _mm_norm_res_kernel(*refs, steps, n_chunk, m_chunk, with_next):
    n_parts = len(steps)
    a_refs = refs[:n_parts]
    b_ref, w_ref, r_ref = refs[n_parts:n_parts + 3]
    rest = refs[n_parts + 3:]
    wn_ref, o_ref, h_ref = rest if with_next else (None, rest[0], None)
    k = pl.program_id(1)
    nk = sum(steps)
    tm, n = o_ref.shape

    @pl.when(k == 0)
    def _():
        o_ref[...] = jnp.zeros_like(o_ref)

    def accumulate(a_ref):
        a = a_ref[...]
        for n0 in range(0, n, n_chunk):
            o_ref[:, n0:n0 + n_chunk] += jnp.dot(a, b_ref[:, n0:n0 + n_chunk],
                                                 preferred_element_type=F32)

    lo = 0
    for a_ref, cnt in zip(a_refs, steps):
        if n_parts == 1:
            accumulate(a_ref)
        else:
            pl.when(jnp.logical_and(k >= lo, k < lo + cnt))(functools.partial(accumulate, a_ref))
        lo += cnt

    @pl.when(k == nk - 1)
    def _():
        for m0 in range(0, tm, m_chunk):
            rows = slice(m0, m0 + m_chunk)
            y = o_ref[rows, :]
            y = y * lax.rsqrt(jnp.mean(y * y, axis=-1, keepdims=True) + NORM_EPS)
            x_new = r_ref[rows, :] + y * w_ref[...]
            o_ref[rows, :] = x_new
            if with_next:
                hn = x_new * lax.rsqrt(jnp.mean(x_new * x_new, axis=-1, keepdims=True) + NORM_EPS)
                h_ref[rows, :] = (hn * wn_ref[...]).astype(h_ref.dtype)


def _matmul_norm_residual(a_parts, b, w, resid, next_w=None, *, tm=1024, tk=1024):
    m = a_parts[0].shape[0]
    n = b.shape[1]
    tm = min(tm, m)
    tk = min([tk] + [a.shape[1] for a in a_parts])
    steps = tuple(a.shape[1] // tk for a in a_parts)
    starts = [sum(steps[:p]) for p in range(len(steps))]
    with_next = next_w is not None

    def a_spec(start, cnt):
        return pl.BlockSpec((tm, tk), lambda i, k: (i, jnp.clip(k - start, 0, cnt - 1)))

    row_spec = pl.BlockSpec((1, n), lambda i, k: (0, 0))
    tile_spec = pl.BlockSpec((tm, n), lambda i, k: (i, 0))
    in_specs = [a_spec(s, c) for s, c in zip(starts, steps)]
    in_specs += [pl.BlockSpec((tk, n), lambda i, k: (k, 0)), row_spec, tile_spec]
    args = list(a_parts) + [b, w.reshape(1, n), resid]
    out_shape = jax.ShapeDtypeStruct((m, n), F32)
    out_specs = tile_spec
    if with_next:
        in_specs.append(row_spec)
        args.append(next_w.reshape(1, n))
        out_shape = (out_shape, jax.ShapeDtypeStruct((m, n), BF16))
        out_specs = (tile_spec, tile_spec)
    return pl.pallas_call(
        functools.partial(_mm_norm_res_kernel, steps=steps, n_chunk=min(n, 512),
                          m_chunk=min(tm, 256), with_next=with_next),
        out_shape=out_shape,
        grid=(m // tm, sum(steps)),
        in_specs=in_specs,
        out_specs=out_specs,
        compiler_params=_cparams(("parallel", "arbitrary")),
        name="matmul_norm_residual",
    )(*args)


def _head_masks(rows):
    lane = lax.broadcasted_iota(jnp.int32, (rows, PAIR), 1)
    return lane < HEAD_DIM


SB_BLOCK = 128
SB_LOOKBACK = 256
SB_GROUP = 4


def _sb_kernel(q_ref, k_ref, v_ref, upper_ref, o_ref, acc_ref, c_ref, *, seq):
    tq, back = SB_BLOCK, SB_LOOKBACK
    span = back + tq
    first = _head_masks(tq)
    row = lax.broadcasted_iota(jnp.int32, (tq, span), 0)
    col = lax.broadcasted_iota(jnp.int32, (tq, span), 1)
    scale = jnp.asarray(HEAD_DIM ** -0.5, BF16)

    def split_heads(q):
        return jnp.where(first, q, jnp.zeros_like(q)), jnp.where(first, jnp.zeros_like(q), q)

    def scores(qh, kb):
        z = lax.dot_general(qh, kb, (((1,), (1,)), ((), ())), preferred_element_type=F32)
        soft = jnp.log(1.0 + jnp.exp(-jnp.abs(z)))
        return jnp.minimum(z, 0.0) - soft, -jnp.maximum(z, 0.0) - soft

    def combine(acc0, acc1):
        return jnp.where(first, acc0, acc1).astype(o_ref.dtype)

    def group(t_block, k_block, layout):
        n_rows = max(k_off for k_off, _ in layout) + span
        k_base = pl.multiple_of(k_block * tq, tq)
        k_all = k_ref[pl.ds(k_base, n_rows), :]
        v_all = v_ref[pl.ds(k_base, n_rows), :]
        live = None
        for u, (k_off, lag) in enumerate(layout):
            rows = pl.ds(pl.multiple_of((t_block + u) * tq, tq), tq)
            q_heads = split_heads(q_ref[rows, :] * scale)
            strict = col < row + lag
            kb, vb = k_all[k_off:k_off + span], v_all[k_off:k_off + span]
            accs = []
            for h in range(2):
                log_beta, log_keep = scores(q_heads[h], kb)
                log_keep = jnp.where(strict, log_keep, 0.0)
                later = jnp.dot(log_keep.astype(BF16), upper_ref[...], preferred_element_type=F32)
                w = jnp.where(strict, jnp.exp(log_beta + later), 0.0)
                accs.append(jnp.dot(w.astype(BF16), vb, preferred_element_type=F32))
                c = jnp.sum(log_keep, axis=1, keepdims=True)
                acc_ref[u, h] = accs[h]
                c_ref[u, h] = c
                live = c if live is None else jnp.maximum(live, c)
            o_ref[rows, :] = combine(accs[0], accs[1])

        @pl.when(jnp.max(live) > SB_LOG_UNDERFLOW)
        def _():
            for u, (k_off, _) in enumerate(layout):
                rows = pl.ds(pl.multiple_of((t_block + u) * tq, tq), tq)
                q_heads = split_heads(q_ref[rows, :] * scale)

                def cond(carry):
                    j, alive = carry
                    return jnp.logical_and(j >= 0, alive)

                def walk(carry):
                    j, _ = carry
                    key_rows = pl.ds(pl.multiple_of(j * tq, tq), tq)
                    kb, vb = k_ref[key_rows, :], v_ref[key_rows, :]
                    for h in range(2):
                        log_beta, log_keep = scores(q_heads[h], kb)
                        later = jnp.dot(log_keep.astype(BF16), upper_ref[0:tq, 0:tq],
                                        preferred_element_type=F32)
                        c = c_ref[u, h]
                        w = jnp.exp(log_beta + later + c)
                        acc_ref[u, h] += jnp.dot(w.astype(BF16), vb, preferred_element_type=F32)
                        c_ref[u, h] = c + jnp.sum(log_keep, axis=1, keepdims=True)
                    return j - 1, jnp.max(c_ref[u]) > SB_LOG_UNDERFLOW

                lax.while_loop(cond, walk, (k_block + k_off // tq - 1,
                                            jnp.max(c_ref[u]) > SB_LOG_UNDERFLOW))
                o_ref[rows, :] = combine(acc_ref[u, 0], acc_ref[u, 1])

    back_blocks = back // tq
    head_layout = [(max(u - back_blocks, 0) * tq, min(u, back_blocks) * tq) for u in range(SB_GROUP)]
    group(0, 0, head_layout)

    def body(n, _):
        group(n * SB_GROUP, n * SB_GROUP - back_blocks, [(u * tq, back) for u in range(SB_GROUP)])
        return 0

    lax.fori_loop(1, seq // (SB_GROUP * tq), body, 0)


def _stick_breaking(proj, bsz, seq, n_heads, q_col, k_col, v_col):
    t = proj.shape[0]
    n_pairs = n_heads // 2
    qb, kb, vb = q_col // PAIR, k_col // PAIR, v_col // PAIR
    span = SB_LOOKBACK + SB_BLOCK
    upper = (jnp.arange(span)[:, None] > jnp.arange(span)[None, :]).astype(BF16)
    seq_spec = lambda off: pl.BlockSpec((seq, PAIR), lambda b, p: (b, off + p))
    return pl.pallas_call(
        functools.partial(_sb_kernel, seq=seq),
        out_shape=jax.ShapeDtypeStruct((t, n_heads * HEAD_DIM), BF16),
        grid=(bsz, n_pairs),
        in_specs=[seq_spec(qb), seq_spec(kb), seq_spec(vb),
                  pl.BlockSpec((span, span), lambda b, p: (0, 0))],
        out_specs=pl.BlockSpec((seq, PAIR), lambda b, p: (b, p)),
        scratch_shapes=[pltpu.VMEM((SB_GROUP, 2, SB_BLOCK, PAIR), F32),
                        pltpu.VMEM((SB_GROUP, 2, SB_BLOCK, 1), F32)],
        compiler_params=_cparams(("parallel", "parallel")),
        name="stick_breaking_attention",
    )(proj, proj, proj, upper)


def _dilated_kernel(slope_ref, q_ref, k_ref, v_ref, o_ref, num_ref, den_ref, max_ref, *, seq, tq,
                    n_sub):
    pair = pl.program_id(1)
    first = _head_masks(tq)
    span = 2 * tq
    group = n_sub * tq
    first_keys = _head_masks(group + tq)
    ones = jnp.ones((group + tq, PAIR), BF16)
    row = lax.broadcasted_iota(jnp.int32, (tq, span), 0)
    col = lax.broadcasted_iota(jnp.int32, (tq, span), 1)
    scale = HEAD_DIM ** -0.5

    for branch, (window, r) in enumerate(DILATED_PATTERNS):
        assert window // r == tq
        per_class = seq // (r * group)
        stride = None if r == 1 else r

        def bias(h, class_start):
            hops = tq + row - col
            valid = jnp.logical_and(hops >= 0, hops <= tq)
            if class_start:
                valid = jnp.logical_and(valid, col >= tq)
            return jnp.where(valid, -(slope_ref[2 * pair + h] * (hops * r).astype(F32)), NEG_INF)

        bias_rest = [bias(h, False) for h in range(2)]
        bias_head = [bias(h, True) for h in range(2)]

        def body(n, _):
            cls = n // per_class
            i = n - cls * per_class
            m0 = i * group
            rows = pl.ds(cls + r * m0, group, stride=stride)
            before = pl.ds(cls + r * jnp.maximum(m0 - tq, 0), tq, stride=stride)
            q_all = (q_ref[rows, :] * scale).astype(BF16)
            k_all = jnp.concatenate([k_ref[before, :], k_ref[rows, :]], axis=0).astype(BF16)
            v_all = jnp.concatenate([v_ref[before, :], v_ref[rows, :]], axis=0).astype(BF16)
            v_heads = (jnp.where(first_keys, v_all, ones), jnp.where(first_keys, ones, v_all))
            nums, dens, maxes = [], [], []
            for u in range(n_sub):
                q = q_all[u * tq:(u + 1) * tq]
                kc = k_all[u * tq:u * tq + span]
                q_heads = (jnp.where(first, q, jnp.zeros_like(q)),
                           jnp.where(first, jnp.zeros_like(q), q))
                res, mxs = [], []
                for h in range(2):
                    z = lax.dot_general(q_heads[h], kc, (((1,), (1,)), ((), ())),
                                        preferred_element_type=F32)
                    b = bias_rest[h] if u else jnp.where(i == 0, bias_head[h], bias_rest[h])
                    logits = z + b
                    m = jnp.max(logits, axis=-1, keepdims=True)
                    p = jnp.exp(logits - m)
                    res.append(jnp.dot(p.astype(BF16), v_heads[h][u * tq:u * tq + span],
                                       preferred_element_type=F32))
                    mxs.append(jnp.broadcast_to(m, (tq, PAIR)))
                nums.append(jnp.where(first, res[0], res[1]))
                dens.append(pltpu.roll(jnp.where(first, res[1], res[0]), HEAD_DIM, 1))
                maxes.append(jnp.where(first, mxs[0], mxs[1]))
            num = jnp.concatenate(nums, axis=0)
            den = jnp.concatenate(dens, axis=0)
            mx = jnp.concatenate(maxes, axis=0)
            if branch == 0:
                num_ref[rows, :] = num
                den_ref[rows, :] = den
                max_ref[rows, :] = mx
            else:
                m_old = max_ref[rows, :]
                m_new = jnp.maximum(m_old, mx)
                w_old = jnp.exp(m_old - m_new)
                w_new = jnp.exp(mx - m_new)
                num_ref[rows, :] = w_old * num_ref[rows, :] + w_new * num
                den_ref[rows, :] = w_old * den_ref[rows, :] + w_new * den
                max_ref[rows, :] = m_new
            return 0

        lax.fori_loop(0, seq // group, body, 0, unroll=2)

    o_ref[...] = (num_ref[...] / den_ref[...]).astype(o_ref.dtype)


def _dilated_attention(qkv, slopes, bsz, seq, n_heads):
    t = qkv.shape[0]
    n_pairs = n_heads // 2
    tq = DILATED_PATTERNS[0][0] // DILATED_PATTERNS[0][1]
    shortest_class = seq // max(r for _, r in DILATED_PATTERNS)
    n_sub = min(DILATED_GROUP, shortest_class // tq)
    spec = lambda off: pl.BlockSpec((seq, PAIR), lambda b, p: (b, off + p))
    return pl.pallas_call(
        functools.partial(_dilated_kernel, seq=seq, tq=tq, n_sub=n_sub),
        out_shape=jax.ShapeDtypeStruct((t, n_heads * HEAD_DIM), BF16),
        grid=(bsz, n_pairs),
        in_specs=[pl.BlockSpec(memory_space=pltpu.SMEM), spec(0), spec(n_pairs), spec(2 * n_pairs)],
        out_specs=pl.BlockSpec((seq, PAIR), lambda b, p: (b, p)),
        scratch_shapes=[pltpu.VMEM((seq, PAIR), F32)] * 3,
        compiler_params=_cparams(("parallel", "parallel")),
        name="dilated_window_attention",
    )(slopes, qkv, qkv, qkv)


def _logf_cumsum_kernel(f_ref, b_ref, o_ref, floor_ref, carry_ref, low_ref, *, tb):
    i = pl.program_id(1)

    @pl.when(i == 0)
    def _():
        carry_ref[...] = jnp.zeros_like(carry_ref)
        low_ref[...] = jnp.zeros_like(low_ref)

    x = f_ref[...] + b_ref[...]
    log_f = jnp.minimum(x, 0.0) - jnp.log1p(jnp.exp(-jnp.abs(x)))
    row = lax.broadcasted_iota(jnp.int32, (tb, tb), 0)
    col = lax.broadcasted_iota(jnp.int32, (tb, tb), 1)
    lower = jnp.where(col <= row, 1.0, 0.0).astype(BF16)
    cum = _split_dot(lower, log_f, 3) + carry_ref[...]
    o_ref[...] = cum
    carry_ref[...] = cum[tb - 1:tb, :]
    low = jnp.minimum(low_ref[...], jnp.min(cum, axis=0, keepdims=True))
    low_ref[...] = low
    floor_ref[pl.ds(i, 1), :] = low


def _logf_cumsum(rest, b_f_row, bsz, seq, col_block, tb):
    t = rest.shape[0]
    nb = seq // tb
    return pl.pallas_call(
        functools.partial(_logf_cumsum_kernel, tb=tb),
        out_shape=(jax.ShapeDtypeStruct((t, LANES), F32),
                   jax.ShapeDtypeStruct((bsz, nb, LANES), F32)),
        grid=(bsz, nb),
        in_specs=[pl.BlockSpec((tb, LANES), lambda b, i: (b * nb + i, col_block)),
                  pl.BlockSpec((1, LANES), lambda b, i: (0, 0))],
        out_specs=(pl.BlockSpec((tb, LANES), lambda b, i: (b * nb + i, 0)),
                   pl.BlockSpec((None, nb, LANES), lambda b, i: (b, 0, 0))),
        scratch_shapes=[pltpu.VMEM((1, LANES), F32), pltpu.VMEM((1, LANES), F32)],
        compiler_params=_cparams(("parallel", "arbitrary")),
        name="log_forget_cumsum",
    )(rest, b_f_row)


def _fox_kernel(q_ref, k_ref, v_ref, fq_ref, fk_ref, floor_ref, o_ref, acc_ref, m_ref, knorm_ref,
                *, tq, seq):
    pair = pl.program_id(1)
    qi = pl.program_id(2)
    reps = tq // LANES
    first = _head_masks(tq)

    @pl.when(qi == 0)
    def _():
        def norms(c, carry):
            kk = k_ref[pl.ds(pl.multiple_of(c * tq, tq), tq), :].astype(F32)
            sq = kk * kk
            n0 = jnp.sum(jnp.where(first, sq, 0.0), axis=1, keepdims=True)
            n1 = jnp.sum(jnp.where(first, 0.0, sq), axis=1, keepdims=True)
            return jnp.maximum(carry[0], n0), jnp.maximum(carry[1], n1)

        zero = jnp.zeros((tq, 1), F32)
        n0, n1 = lax.fori_loop(0, seq // tq, norms, (zero, zero))
        knorm_ref[0] = jnp.sqrt(jnp.max(n0))
        knorm_ref[1] = jnp.sqrt(jnp.max(n1))

    q = q_ref[...] * jnp.asarray(HEAD_DIM ** -0.5, BF16)
    q_heads = (jnp.where(first, q, jnp.zeros_like(q)), jnp.where(first, jnp.zeros_like(q), q))
    fq_all = fq_ref[...]
    lane = lax.broadcasted_iota(jnp.int32, (tq, LANES), 1)
    f_q = [jnp.broadcast_to(jnp.sum(jnp.where(lane == 2 * pair + h, fq_all, 0.0), axis=1,
                                    keepdims=True), (tq, LANES)) for h in range(2)]
    row = lax.broadcasted_iota(jnp.int32, (tq, tq), 0)
    col = lax.broadcasted_iota(jnp.int32, (tq, tq), 1)
    causal = col <= row
    ones = jnp.ones((tq, PAIR), BF16)

    acc_ref[...] = jnp.zeros_like(acc_ref)
    m_ref[...] = jnp.full_like(m_ref, NEG_INF)

    def step(j, masked):
        k0 = pl.multiple_of(j * tq, tq)
        kb = k_ref[pl.ds(k0, tq), :]
        vb = v_ref[pl.ds(k0, tq), :]
        v_heads = (jnp.where(first, vb, ones), jnp.where(first, ones, vb))
        for h in range(2):
            f_k = fk_ref[pl.ds(2 * pair + h, 1), pl.ds(k0, tq)]
            s = lax.dot_general(q_heads[h], kb, (((1,), (1,)), ((), ())),
                                preferred_element_type=F32)
            s = s + jnp.tile(f_q[h], (1, reps)) - f_k
            if masked:
                s = jnp.where(causal, s, NEG_INF)
            m_old = m_ref[h]
            m_new = jnp.maximum(m_old, jnp.max(s, axis=-1, keepdims=True))
            alpha = jnp.exp(m_old - m_new)
            p = jnp.exp(s - jnp.tile(m_new, (1, reps)))
            acc_ref[h] = alpha * acc_ref[h] + jnp.dot(p.astype(BF16), v_heads[h],
                                                      preferred_element_type=F32)
            m_ref[h] = m_new

    step(qi, True)

    floors = floor_ref[...]
    blk = lax.broadcasted_iota(jnp.int32, floors.shape, 0)
    head_lane = lax.broadcasted_iota(jnp.int32, floors.shape, 1)
    skip = []
    for h in range(2):
        q32 = q_heads[h].astype(F32)
        q_norm = jnp.sqrt(jnp.sum(q32 * q32, axis=1, keepdims=True))
        reach = f_q[h] + q_norm * (knorm_ref[h] * FOX_NORM_SLACK) - m_ref[h]
        limit = jnp.max(reach) + FOX_LOG_UNDERFLOW
        dead = jnp.logical_and(jnp.logical_and(head_lane == 2 * pair + h, blk < qi), floors > limit)
        skip.append(jnp.sum(jnp.where(dead, 1, 0)))
    start = jnp.minimum(skip[0], skip[1])

    def body(j, _):
        step(j, False)
        return 0

    lax.fori_loop(start, qi, body, 0)
    outs = [acc_ref[h] / pltpu.roll(acc_ref[h], HEAD_DIM, 1) for h in range(2)]
    o_ref[...] = jnp.where(first, outs[0], outs[1]).astype(o_ref.dtype)


FOX_BLOCK = 512


def _forgetting_attention(qkv, f_cum, f_cum_t, f_floor, bsz, seq, n_heads):
    t = qkv.shape[0]
    tq = FOX_BLOCK
    n_pairs = n_heads // 2
    nq = seq // tq
    return pl.pallas_call(
        functools.partial(_fox_kernel, tq=tq, seq=seq),
        out_shape=jax.ShapeDtypeStruct((t, n_heads * HEAD_DIM), BF16),
        grid=(bsz, n_pairs, nq),
        in_specs=[pl.BlockSpec((tq, PAIR), lambda b, p, i: (b * nq + i, p)),
                  pl.BlockSpec((seq, PAIR), lambda b, p, i: (b, n_pairs + p)),
                  pl.BlockSpec((seq, PAIR), lambda b, p, i: (b, 2 * n_pairs + p)),
                  pl.BlockSpec((tq, LANES), lambda b, p, i: (b * nq + i, 0)),
                  pl.BlockSpec((None, n_heads, seq), lambda b, p, i: (b, 0, 0)),
                  pl.BlockSpec((None, nq, LANES), lambda b, p, i: (b, 0, 0))],
        out_specs=pl.BlockSpec((tq, PAIR), lambda b, p, i: (b * nq + i, p)),
        scratch_shapes=[pltpu.VMEM((2, tq, PAIR), F32), pltpu.VMEM((2, tq, LANES), F32),
                        pltpu.SMEM((2,), F32)],
        compiler_params=_cparams(("arbitrary", "arbitrary", "arbitrary")),
        name="forgetting_attention",
    )(qkv, qkv, qkv, f_cum, f_cum_t, f_floor)


def _conv_silu_kernel(x_ref, halo_ref, w_ref, b_ref, o_ref, *, tb, blocks_per_seq, width):
    i = pl.program_id(0)
    x = x_ref[...]
    halo = jnp.where(i % blocks_per_seq == 0, jnp.zeros_like(halo_ref[...]), halo_ref[...])
    xx = jnp.concatenate([halo, x], axis=0)
    w = w_ref[...]
    y = b_ref[...] + w[width - 1:width, :] * x
    for tap in range(width - 1):
        shift = width - 1 - tap
        y = y + w[tap:tap + 1, :] * xx[8 - shift:8 - shift + tb, :]
    o_ref[...] = y * (1.0 / (1.0 + jnp.exp(-y)))


def _conv_silu(rest, conv_w, conv_b, seq, col0, n_ch, tb=512):
    t = rest.shape[0]
    width = conv_w.shape[0]
    tc = math.gcd(col0, n_ch)
    cb0 = col0 // tc
    hb = tb // 8
    return pl.pallas_call(
        functools.partial(_conv_silu_kernel, tb=tb, blocks_per_seq=seq // tb, width=width),
        out_shape=jax.ShapeDtypeStruct((t, n_ch), F32),
        grid=(t // tb, n_ch // tc),
        in_specs=[pl.BlockSpec((tb, tc), lambda i, j: (i, cb0 + j)),
                  pl.BlockSpec((8, tc), lambda i, j: (jnp.maximum(i * hb - 1, 0), cb0 + j)),
                  pl.BlockSpec((width, tc), lambda i, j: (0, j)),
                  pl.BlockSpec((1, tc), lambda i, j: (0, j))],
        out_specs=pl.BlockSpec((tb, tc), lambda i, j: (i, j)),
        compiler_params=_cparams(("parallel", "parallel")),
        name="causal_conv_silu",
    )(rest, rest, conv_w, conv_b.reshape(1, n_ch))


def _ssd_kernel(xc_ref, z_ref, dt_ref, dtt_ref, bias_row_ref, bias_col_ref, alog_row_ref,
                alog_col_ref, dskip_ref, gate_ref, o_ref, state_ref, *, n_heads, d_inner):
    q = SSD_CHUNK
    n = SSM_STATE
    hpg = n_heads // SSM_GROUPS
    gw = hpg * HEAD_DIM

    @pl.when(pl.program_id(1) == 0)
    def _():
        state_ref[...] = jnp.zeros_like(state_ref)

    row = lax.broadcasted_iota(jnp.int32, (q, q), 0)
    col = lax.broadcasted_iota(jnp.int32, (q, q), 1)
    causal = col <= row
    lower = jnp.where(causal, 1.0, 0.0).astype(BF16)
    upper = jnp.where(row <= col, 1.0, 0.0).astype(BF16)
    er = lax.broadcasted_iota(jnp.int32, (LANES, d_inner), 0)
    ec = lax.broadcasted_iota(jnp.int32, (LANES, d_inner), 1)
    expand = jnp.where(ec // HEAD_DIM == er, 1.0, 0.0).astype(BF16)
    first = _head_masks(q)

    dt = _softplus(dt_ref[...] + bias_row_ref[...])
    a_cum = _split_dot(lower, dt * (-jnp.exp(alog_row_ref[...])), 3)
    a_last = a_cum[q - 1:q, :]
    dt_t = _softplus(dtt_ref[...] + bias_col_ref[...])
    a_cum_t = _dot_split(dt_t * (-jnp.exp(alog_col_ref[...])), upper, 3)

    dt_x = _dot_split(dt, expand, 2)
    decay_in_x = _dot_split(jnp.exp(a_cum), expand, 2)
    decay_out_x = _dot_split(jnp.exp(a_last - a_cum), expand, 2)
    chunk_decay_x = _dot_split(jnp.broadcast_to(jnp.exp(a_last), (8, LANES)), expand, 2)[0:1, :]

    xs = xc_ref[:, 0:d_inner]
    xdt = xs * dt_x
    xdt_bf = xdt.astype(BF16)
    xend_bf = (xdt * decay_out_x).astype(BF16)

    y_groups = []
    for g in range(SSM_GROUPS):
        b_g = xc_ref[:, d_inner + g * n:d_inner + (g + 1) * n].astype(BF16)
        c_g = xc_ref[:, d_inner + (SSM_GROUPS + g) * n:d_inner + (SSM_GROUPS + g + 1) * n].astype(BF16)
        cb = lax.dot_general(c_g, b_g, (((1,), (1,)), ((), ())), preferred_element_type=F32)
        state = state_ref[g]
        y_off = jnp.dot(c_g, state.astype(BF16), preferred_element_type=F32)
        y_g = y_off * decay_in_x[:, g * gw:(g + 1) * gw]
        diag_pairs = []
        for pr in range(hpg // 2):
            lo = g * gw + pr * PAIR
            x_pair = xdt_bf[:, lo:lo + PAIR]
            acc = None
            for hh in range(2):
                h = g * hpg + pr * 2 + hh
                seg = a_cum[:, h:h + 1] - a_cum_t[h:h + 1, :]
                m = (cb * jnp.where(causal, jnp.exp(seg), 0.0)).astype(BF16)
                x_h = jnp.where(first, x_pair, jnp.zeros_like(x_pair)) if hh == 0 else \
                    jnp.where(first, jnp.zeros_like(x_pair), x_pair)
                d = jnp.dot(m, x_h, preferred_element_type=F32)
                acc = d if acc is None else acc + d
            diag_pairs.append(acc)
        y_groups.append(y_g + jnp.concatenate(diag_pairs, axis=1))
        new_state = lax.dot_general(b_g, xend_bf[:, g * gw:(g + 1) * gw], (((0,), (0,)), ((), ())),
                                    preferred_element_type=F32)
        state_ref[g] = state * chunk_decay_x[:, g * gw:(g + 1) * gw] + new_state

    y = jnp.concatenate(y_groups, axis=1) + dskip_ref[...] * xs
    z = z_ref[...]
    gated = y * (z * (1.0 / (1.0 + jnp.exp(-z))))
    gn = d_inner // SSM_GROUPS
    normed = []
    for g in range(SSM_GROUPS):
        gg = gated[:, g * gn:(g + 1) * gn]
        normed.append(gg * lax.rsqrt(jnp.mean(gg * gg, axis=-1, keepdims=True) + NORM_EPS))
    o_ref[...] = (jnp.concatenate(normed, axis=1) * gate_ref[...]).astype(o_ref.dtype)


def _ssd(conv_out, rest, dt_t, dt_bias, a_log, d_skip, gate_norm, bsz, seq, n_heads, d_inner,
         z_col, dt_col_block):
    t = conv_out.shape[0]
    nc = seq // SSD_CHUNK
    conv_ch = conv_out.shape[1]
    pad = LANES - n_heads
    row = lambda v: jnp.pad(v, (0, pad)).reshape(1, LANES)
    colv = lambda v: v.reshape(n_heads, 1)
    zb = z_col // d_inner
    const = lambda shape: pl.BlockSpec(shape, lambda b, c: (0,) * len(shape))
    return pl.pallas_call(
        functools.partial(_ssd_kernel, n_heads=n_heads, d_inner=d_inner),
        out_shape=jax.ShapeDtypeStruct((t, d_inner), BF16),
        grid=(bsz, nc),
        in_specs=[pl.BlockSpec((SSD_CHUNK, conv_ch), lambda b, c: (b * nc + c, 0)),
                  pl.BlockSpec((SSD_CHUNK, d_inner), lambda b, c: (b * nc + c, zb)),
                  pl.BlockSpec((SSD_CHUNK, LANES), lambda b, c: (b * nc + c, dt_col_block)),
                  pl.BlockSpec((None, n_heads, SSD_CHUNK), lambda b, c: (b, 0, c)),
                  const((1, LANES)), const((n_heads, 1)), const((1, LANES)), const((n_heads, 1)),
                  const((1, d_inner)), const((1, d_inner))],
        out_specs=pl.BlockSpec((SSD_CHUNK, d_inner), lambda b, c: (b * nc + c, 0)),
        scratch_shapes=[pltpu.VMEM((SSM_GROUPS, SSM_STATE, d_inner // SSM_GROUPS), F32)],
        compiler_params=_cparams(("parallel", "arbitrary")),
        name="ssd_chunk_scan",
    )(conv_out, rest, rest, dt_t, row(dt_bias), colv(dt_bias), row(a_log), colv(a_log),
      jnp.repeat(d_skip, HEAD_DIM).reshape(1, d_inner), gate_norm.reshape(1, d_inner))


def _even_mixer(h, w_in, w_out, post_w, resid, next_w, bsz, seq):
    d = h.shape[1]
    n_heads = d // (2 * HEAD_DIM)
    hw = n_heads * HEAD_DIM
    qkv_a = _matmul_f32_weight(h, w_in, 0, 3 * hw, BF16)
    qkv_b = _matmul_f32_weight(h, w_in, 3 * hw, 3 * hw, F32, tm=1024)
    oa = _stick_breaking(qkv_a, bsz, seq, n_heads, 0, hw, 2 * hw)
    slopes = jnp.exp2(-ALIBI_MAX_EXP * jnp.arange(1, n_heads + 1, dtype=F32) / n_heads)
    ob = _dilated_attention(qkv_b, slopes, bsz, seq, n_heads)
    return _matmul_norm_residual([oa, ob], w_out.astype(BF16), post_w, resid, next_w, tk=512)


def _odd_mixer(h, w_in, b_f, conv_w, conv_b, dt_bias, a_log, d_skip, gate_norm, w_out, post_w,
               resid, next_w, bsz, seq):
    d = h.shape[1]
    n_heads = d // (2 * HEAD_DIM)
    hw = n_heads * HEAD_DIM
    d_inner = hw
    conv_ch = d_inner + 2 * SSM_GROUPS * SSM_STATE
    pad = LANES - n_heads
    c0 = 3 * hw
    f_w = w_in[:, c0:c0 + n_heads]
    z_w = w_in[:, c0 + n_heads:c0 + n_heads + d_inner]
    xbc_w = w_in[:, c0 + n_heads + d_inner:c0 + n_heads + d_inner + conv_ch]
    dt_w = w_in[:, c0 + n_heads + d_inner + conv_ch:]
    w_rest = jnp.concatenate([z_w, xbc_w, jnp.pad(f_w, ((0, 0), (0, pad))),
                              jnp.pad(dt_w, ((0, 0), (0, pad)))], axis=1).astype(BF16)
    qkv = _matmul_f32_weight(h, w_in, 0, c0, BF16)
    rest = _matmul(h, w_rest, F32)
    f_block = (d_inner + conv_ch) // LANES
    dt_block = f_block + 1

    f_cum, f_floor = _logf_cumsum(rest, jnp.pad(b_f, (0, pad)).reshape(1, LANES), bsz, seq, f_block,
                                  FOX_BLOCK)
    f_cum_t = f_cum[:, :n_heads].reshape(bsz, seq, n_heads).transpose(0, 2, 1)
    oc = _forgetting_attention(qkv, f_cum, f_cum_t, f_floor, bsz, seq, n_heads)

    conv_out = _conv_silu(rest, conv_w, conv_b, seq, d_inner, conv_ch)
    dt_col = dt_block * LANES
    dt_t = rest[:, dt_col:dt_col + n_heads].reshape(bsz, seq, n_heads).transpose(0, 2, 1)
    y = _ssd(conv_out, rest, dt_t, dt_bias, a_log, d_skip, gate_norm, bsz, seq, n_heads, d_inner,
             0, dt_block)
    return _matmul_norm_residual([oc, y], w_out.astype(BF16), post_w, resid, next_w, tk=512)


def _mlp(h, w_up, w_down, post_w, resid, next_w):
    hidden = _matmul_f32_weight(h, w_up, 0, w_up.shape[1], BF16, relu2=True)
    return _matmul_norm_residual([hidden], w_down.astype(BF16), post_w, resid, next_w)


def kernel(x, mix_norm_pre, mix_norm_post, mlp_norm_pre, mlp_norm_post, ab_w_in, ab_w_out,
           cd_w_in, cd_b_f, cd_conv_w, cd_conv_b, cd_dt_bias, cd_a_log, cd_d_skip,
           cd_gate_norm, cd_w_out, mlp_w_up, mlp_w_down):
    bsz, seq, d = x.shape
    depth = mix_norm_pre.shape[0]
    xf = x.reshape(bsz * seq, d)
    h = _rmsnorm(xf, mix_norm_pre[0])
    for layer in range(depth):
        i = layer // 2
        if layer % 2 == 0:
            xf, h = _even_mixer(h, ab_w_in[i], ab_w_out[i], mix_norm_post[layer], xf,
                                mlp_norm_pre[layer], bsz, seq)
        else:
            xf, h = _odd_mixer(h, cd_w_in[i], cd_b_f[i], cd_conv_w[i], cd_conv_b[i], cd_dt_bias[i],
                               cd_a_log[i], cd_d_skip[i], cd_gate_norm[i], cd_w_out[i],
                               mix_norm_post[layer], xf, mlp_norm_pre[layer], bsz, seq)
        if layer + 1 < depth:
            xf, h = _mlp(h, mlp_w_up[layer], mlp_w_down[layer], mlp_norm_post[layer], xf,
                         mix_norm_pre[layer + 1])
        else:
            xf = _mlp(h, mlp_w_up[layer], mlp_w_down[layer], mlp_norm_post[layer], xf, None)
    return xf.reshape(bsz, seq, d)
```

```python
import functools
import math

import jax
import jax.numpy as jnp
from jax import lax
from jax.experimental import pallas as pl
from jax.experimental.pallas import tpu as pltpu

F32 = jnp.float32
BF16 = jnp.bfloat16

LANES = 128
HEAD_DIM = 64
PAIR = 2 * HEAD_DIM
NORM_EPS = 1e-6
NEG_INF = -1e30
DILATED_PATTERNS = ((128, 1), (512, 4), (2048, 16))
DILATED_GROUP = 4
ALIBI_MAX_EXP = 8.0
SSM_GROUPS = 4
SSM_STATE = 128
SSD_CHUNK = 128
SB_LOG_UNDERFLOW = -104.0
FOX_LOG_UNDERFLOW = 104.0
FOX_NORM_SLACK = 1.001
VMEM_LIMIT_BYTES = 56 * 1024 * 1024


def _cparams(sem):
    return pltpu.CompilerParams(dimension_semantics=sem, vmem_limit_bytes=VMEM_LIMIT_BYTES)


def _softplus(x):
    return jnp.maximum(x, 0.0) + jnp.log1p(jnp.exp(-jnp.abs(x)))


def _split_bf16(x, parts):
    out = []
    rem = x
    for _ in range(parts):
        hi = rem.astype(BF16)
        out.append(hi)
        rem = rem - hi.astype(F32)
    return out


def _dot_split(x, m, parts):
    acc = None
    for t in _split_bf16(x, parts):
        d = jnp.dot(t, m, preferred_element_type=F32)
        acc = d if acc is None else acc + d
    return acc


def _split_dot(m, x, parts):
    acc = None
    for t in _split_bf16(x, parts):
        d = jnp.dot(m, t, preferred_element_type=F32)
        acc = d if acc is None else acc + d
    return acc


def _rmsnorm_kernel(x_ref, w_ref, o_ref):
    x = x_ref[...]
    y = x * lax.rsqrt(jnp.mean(x * x, axis=-1, keepdims=True) + NORM_EPS)
    o_ref[...] = (y * w_ref[...]).astype(o_ref.dtype)


def _rmsnorm(x2d, w, tm=512):
    t, d = x2d.shape
    return pl.pallas_call(
        _rmsnorm_kernel,
        out_shape=jax.ShapeDtypeStruct((t, d), BF16),
        grid=(t // tm,),
        in_specs=[pl.BlockSpec((tm, d), lambda i: (i, 0)),
                  pl.BlockSpec((1, d), lambda i: (0, 0))],
        out_specs=pl.BlockSpec((tm, d), lambda i: (i, 0)),
        compiler_params=_cparams(("parallel",)),
        name="rmsnorm",
    )(x2d, w.reshape(1, d))


def _mm_kernel(a_ref, b_ref, o_ref, *, relu2):
    acc = jnp.dot(a_ref[...], b_ref[...], preferred_element_type=F32)
    if relu2:
        acc = jnp.square(jnp.maximum(acc, 0.0))
    o_ref[...] = acc.astype(o_ref.dtype)


def _matmul(a, b, out_dtype, *, relu2=False, tm=1024, tn=1792):
    m, kk = a.shape
    n = b.shape[1]
    tm, tn = min(tm, m), min(tn, n)
    while n % tn:
        tn -= LANES
    return pl.pallas_call(
        functools.partial(_mm_kernel, relu2=relu2),
        out_shape=jax.ShapeDtypeStruct((m, n), out_dtype),
        grid=(m // tm, n // tn),
        in_specs=[pl.BlockSpec((tm, kk), lambda i, j: (i, 0)),
                  pl.BlockSpec((kk, tn), lambda i, j: (0, j))],
        out_specs=pl.BlockSpec((tm, tn), lambda i, j: (i, j)),
        compiler_params=_cparams(("parallel", "parallel")),
        name="matmul",
    )(a, b)


def _mm_f32w_kernel(a_ref, b_ref, o_ref, w_ref, *, relu2):
    @pl.when(pl.program_id(1) == 0)
    def _():
        w_ref[...] = b_ref[...].astype(BF16)

    acc = jnp.dot(a_ref[...], w_ref[...], preferred_element_type=F32)
    if relu2:
        acc = jnp.square(jnp.maximum(acc, 0.0))
    o_ref[...] = acc.astype(o_ref.dtype)


def _matmul_f32_weight(a, w, layer, col0, n, out_dtype, *, relu2=False, tm=2048, tn=1024):
    m, kk = a.shape
    tm = min(tm, m)
    g = math.gcd(n, col0)
    tn = max(t for t in range(LANES, min(tn, g) + 1, LANES) if g % t == 0)
    jb = col0 // tn
    return pl.pallas_call(
        functools.partial(_mm_f32w_kernel, relu2=relu2),
        out_shape=jax.ShapeDtypeStruct((m, n), out_dtype),
        grid=(n // tn, m // tm),
        in_specs=[pl.BlockSpec((tm, kk), lambda j, i: (i, 0)),
                  pl.BlockSpec((None, kk, tn), lambda j, i: (layer, 0, jb + j))],
        out_specs=pl.BlockSpec((tm, tn), lambda j, i: (i, j)),
        scratch_shapes=[pltpu.VMEM((kk, tn), BF16)],
        compiler_params=_cparams(("arbitrary", "arbitrary")),
        name="matmul_f32_weight",
    )(a, w)


def _mm_norm_res_kernel(*refs, steps, n_chunk, m_chunk, with_next):
    n_parts = len(steps)
    a_refs = refs[:n_parts]
    b_ref, w_ref, r_ref = refs[n_parts:n_parts + 3]
    rest = refs[n_parts + 3:]
    wn_ref, o_ref, h_ref = rest if with_next else (None, rest[0], None)
    k = pl.program_id(1)
    nk = sum(steps)
    tm, n = o_ref.shape

    @pl.when(k == 0)
    def _():
        o_ref[...] = jnp.zeros_like(o_ref)

    def accumulate(a_ref):
        a = a_ref[...]
        for n0 in range(0, n, n_chunk):
            o_ref[:, n0:n0 + n_chunk] += jnp.dot(a, b_ref[:, n0:n0 + n_chunk],
                                                 preferred_element_type=F32)

    lo = 0
    for a_ref, cnt in zip(a_refs, steps):
        if n_parts == 1:
            accumulate(a_ref)
        else:
            pl.when(jnp.logical_and(k >= lo, k < lo + cnt))(functools.partial(accumulate, a_ref))
        lo += cnt

    @pl.when(k == nk - 1)
    def _():
        for m0 in range(0, tm, m_chunk):
            rows = slice(m0, m0 + m_chunk)
            y = o_ref[rows, :]
            y = y * lax.rsqrt(jnp.mean(y * y, axis=-1, keepdims=True) + NORM_EPS)
            x_new = r_ref[rows, :] + y * w_ref[...]
            o_ref[rows, :] = x_new
            if with_next:
                hn = x_new * lax.rsqrt(jnp.mean(x_new * x_new, axis=-1, keepdims=True) + NORM_EPS)
                h_ref[rows, :] = (hn * wn_ref[...]).astype(h_ref.dtype)


def _matmul_norm_residual(a_parts, b, layer, w, resid, next_w=None, *, tm=1024, tk=1024):
    m = a_parts[0].shape[0]
    n = b.shape[2]
    tm = min(tm, m)
    tk = min([tk] + [a.shape[1] for a in a_parts])
    steps = tuple(a.shape[1] // tk for a in a_parts)
    starts = [sum(steps[:p]) for p in range(len(steps))]
    with_next = next_w is not None

    def a_spec(start, cnt):
        return pl.BlockSpec((tm, tk), lambda i, k: (i, jnp.clip(k - start, 0, cnt - 1)))

    row_spec = pl.BlockSpec((1, n), lambda i, k: (0, 0))
    tile_spec = pl.BlockSpec((tm, n), lambda i, k: (i, 0))
    in_specs = [a_spec(s, c) for s, c in zip(starts, steps)]
    in_specs += [pl.BlockSpec((None, tk, n), lambda i, k: (layer, k, 0)), row_spec, tile_spec]
    args = list(a_parts) + [b, w.reshape(1, n), resid]
    out_shape = jax.ShapeDtypeStruct((m, n), F32)
    out_specs = tile_spec
    if with_next:
        in_specs.append(row_spec)
        args.append(next_w.reshape(1, n))
        out_shape = (out_shape, jax.ShapeDtypeStruct((m, n), BF16))
        out_specs = (tile_spec, tile_spec)
    return pl.pallas_call(
        functools.partial(_mm_norm_res_kernel, steps=steps, n_chunk=min(n, 512),
                          m_chunk=min(tm, 256), with_next=with_next),
        out_shape=out_shape,
        grid=(m // tm, sum(steps)),
        in_specs=in_specs,
        out_specs=out_specs,
        compiler_params=_cparams(("parallel", "arbitrary")),
        name="matmul_norm_residual",
    )(*args)


def _head_masks(rows):
    lane = lax.broadcasted_iota(jnp.int32, (rows, PAIR), 1)
    return lane < HEAD_DIM


SB_BLOCK = 128
SB_LOOKBACK = 256
SB_GROUP = 4


def _sb_kernel(q_ref, k_ref, v_ref, upper_ref, o_ref, acc_ref, c_ref, *, seq):
    tq, back = SB_BLOCK, SB_LOOKBACK
    span = back + tq
    first = _head_masks(tq)
    row = lax.broadcasted_iota(jnp.int32, (tq, span), 0)
    col = lax.broadcasted_iota(jnp.int32, (tq, span), 1)
    scale = jnp.asarray(HEAD_DIM ** -0.5, BF16)

    def split_heads(q):
        return jnp.where(first, q, jnp.zeros_like(q)), jnp.where(first, jnp.zeros_like(q), q)

    def scores(qh, kb):
        z = lax.dot_general(qh, kb, (((1,), (1,)), ((), ())), preferred_element_type=F32)
        soft = jnp.log(1.0 + jnp.exp(-jnp.abs(z)))
        return jnp.minimum(z, 0.0) - soft, -jnp.maximum(z, 0.0) - soft

    def combine(acc0, acc1):
        return jnp.where(first, acc0, acc1).astype(o_ref.dtype)

    def group(t_block, k_block, layout):
        n_rows = max(k_off for k_off, _ in layout) + span
        k_base = pl.multiple_of(k_block * tq, tq)
        k_all = k_ref[pl.ds(k_base, n_rows), :]
        v_all = v_ref[pl.ds(k_base, n_rows), :]
        live = None
        for u, (k_off, lag) in enumerate(layout):
            rows = pl.ds(pl.multiple_of((t_block + u) * tq, tq), tq)
            q_heads = split_heads(q_ref[rows, :] * scale)
            strict = col < row + lag
            kb, vb = k_all[k_off:k_off + span], v_all[k_off:k_off + span]
            accs = []
            for h in range(2):
                log_beta, log_keep = scores(q_heads[h], kb)
                log_keep = jnp.where(strict, log_keep, 0.0)
                later = jnp.dot(log_keep.astype(BF16), upper_ref[...], preferred_element_type=F32)
                w = jnp.where(strict, jnp.exp(log_beta + later), 0.0)
                accs.append(jnp.dot(w.astype(BF16), vb, preferred_element_type=F32))
                c = jnp.sum(log_keep, axis=1, keepdims=True)
                acc_ref[u, h] = accs[h]
                c_ref[u, h] = c
                live = c if live is None else jnp.maximum(live, c)
            o_ref[rows, :] = combine(accs[0], accs[1])

        @pl.when(jnp.max(live) > SB_LOG_UNDERFLOW)
        def _():
            for u, (k_off, _) in enumerate(layout):
                rows = pl.ds(pl.multiple_of((t_block + u) * tq, tq), tq)
                q_heads = split_heads(q_ref[rows, :] * scale)

                def cond(carry):
                    j, alive = carry
                    return jnp.logical_and(j >= 0, alive)

                def walk(carry):
                    j, _ = carry
                    key_rows = pl.ds(pl.multiple_of(j * tq, tq), tq)
                    kb, vb = k_ref[key_rows, :], v_ref[key_rows, :]
                    for h in range(2):
                        log_beta, log_keep = scores(q_heads[h], kb)
                        later = jnp.dot(log_keep.astype(BF16), upper_ref[0:tq, 0:tq],
                                        preferred_element_type=F32)
                        c = c_ref[u, h]
                        w = jnp.exp(log_beta + later + c)
                        acc_ref[u, h] += jnp.dot(w.astype(BF16), vb, preferred_element_type=F32)
                        c_ref[u, h] = c + jnp.sum(log_keep, axis=1, keepdims=True)
                    return j - 1, jnp.max(c_ref[u]) > SB_LOG_UNDERFLOW

                lax.while_loop(cond, walk, (k_block + k_off // tq - 1,
                                            jnp.max(c_ref[u]) > SB_LOG_UNDERFLOW))
                o_ref[rows, :] = combine(acc_ref[u, 0], acc_ref[u, 1])

    back_blocks = back // tq
    head_layout = [(max(u - back_blocks, 0) * tq, min(u, back_blocks) * tq) for u in range(SB_GROUP)]
    group(0, 0, head_layout)

    def body(n, _):
        group(n * SB_GROUP, n * SB_GROUP - back_blocks, [(u * tq, back) for u in range(SB_GROUP)])
        return 0

    lax.fori_loop(1, seq // (SB_GROUP * tq), body, 0)


def _stick_breaking(proj, bsz, seq, n_heads, q_col, k_col, v_col):
    t = proj.shape[0]
    n_pairs = n_heads // 2
    qb, kb, vb = q_col // PAIR, k_col // PAIR, v_col // PAIR
    span = SB_LOOKBACK + SB_BLOCK
    upper = (jnp.arange(span)[:, None] > jnp.arange(span)[None, :]).astype(BF16)
    seq_spec = lambda off: pl.BlockSpec((seq, PAIR), lambda b, p: (b, off + p))
    return pl.pallas_call(
        functools.partial(_sb_kernel, seq=seq),
        out_shape=jax.ShapeDtypeStruct((t, n_heads * HEAD_DIM), BF16),
        grid=(bsz, n_pairs),
        in_specs=[seq_spec(qb), seq_spec(kb), seq_spec(vb),
                  pl.BlockSpec((span, span), lambda b, p: (0, 0))],
        out_specs=pl.BlockSpec((seq, PAIR), lambda b, p: (b, p)),
        scratch_shapes=[pltpu.VMEM((SB_GROUP, 2, SB_BLOCK, PAIR), F32),
                        pltpu.VMEM((SB_GROUP, 2, SB_BLOCK, 1), F32)],
        compiler_params=_cparams(("parallel", "parallel")),
        name="stick_breaking_attention",
    )(proj, proj, proj, upper)


def _dilated_kernel(slope_ref, q_ref, k_ref, v_ref, o_ref, num_ref, den_ref, max_ref, *, seq, tq,
                    n_sub):
    pair = pl.program_id(1)
    first = _head_masks(tq)
    span = 2 * tq
    group = n_sub * tq
    first_keys = _head_masks(group + tq)
    ones = jnp.ones((group + tq, PAIR), BF16)
    row = lax.broadcasted_iota(jnp.int32, (tq, span), 0)
    col = lax.broadcasted_iota(jnp.int32, (tq, span), 1)
    scale = HEAD_DIM ** -0.5

    for branch, (window, r) in enumerate(DILATED_PATTERNS):
        assert window // r == tq
        per_class = seq // (r * group)
        stride = None if r == 1 else r

        def bias(h, class_start):
            hops = tq + row - col
            valid = jnp.logical_and(hops >= 0, hops <= tq)
            if class_start:
                valid = jnp.logical_and(valid, col >= tq)
            return jnp.where(valid, -(slope_ref[2 * pair + h] * (hops * r).astype(F32)), NEG_INF)

        bias_rest = [bias(h, False) for h in range(2)]
        bias_head = [bias(h, True) for h in range(2)]

        def body(n, _):
            cls = n // per_class
            i = n - cls * per_class
            m0 = i * group
            rows = pl.ds(cls + r * m0, group, stride=stride)
            before = pl.ds(cls + r * jnp.maximum(m0 - tq, 0), tq, stride=stride)
            q_all = (q_ref[rows, :] * scale).astype(BF16)
            k_all = jnp.concatenate([k_ref[before, :], k_ref[rows, :]], axis=0).astype(BF16)
            v_all = jnp.concatenate([v_ref[before, :], v_ref[rows, :]], axis=0).astype(BF16)
            v_heads = (jnp.where(first_keys, v_all, ones), jnp.where(first_keys, ones, v_all))
            nums, dens, maxes = [], [], []
            for u in range(n_sub):
                q = q_all[u * tq:(u + 1) * tq]
                kc = k_all[u * tq:u * tq + span]
                q_heads = (jnp.where(first, q, jnp.zeros_like(q)),
                           jnp.where(first, jnp.zeros_like(q), q))
                res, mxs = [], []
                for h in range(2):
                    z = lax.dot_general(q_heads[h], kc, (((1,), (1,)), ((), ())),
                                        preferred_element_type=F32)
                    b = bias_rest[h] if u else jnp.where(i == 0, bias_head[h], bias_rest[h])
                    logits = z + b
                    m = jnp.max(logits, axis=-1, keepdims=True)
                    p = jnp.exp(logits - m)
                    res.append(jnp.dot(p.astype(BF16), v_heads[h][u * tq:u * tq + span],
                                       preferred_element_type=F32))
                    mxs.append(jnp.broadcast_to(m, (tq, PAIR)))
                nums.append(jnp.where(first, res[0], res[1]))
                dens.append(pltpu.roll(jnp.where(first, res[1], res[0]), HEAD_DIM, 1))
                maxes.append(jnp.where(first, mxs[0], mxs[1]))
            num = jnp.concatenate(nums, axis=0)
            den = jnp.concatenate(dens, axis=0)
            mx = jnp.concatenate(maxes, axis=0)
            if branch == 0:
                num_ref[rows, :] = num
                den_ref[rows, :] = den
                max_ref[rows, :] = mx
            else:
                m_old = max_ref[rows, :]
                m_new = jnp.maximum(m_old, mx)
                w_old = jnp.exp(m_old - m_new)
                w_new = jnp.exp(mx - m_new)
                num_ref[rows, :] = w_old * num_ref[rows, :] + w_new * num
                den_ref[rows, :] = w_old * den_ref[rows, :] + w_new * den
                max_ref[rows, :] = m_new
            return 0

        lax.fori_loop(0, seq // group, body, 0, unroll=2)

    o_ref[...] = (num_ref[...] / den_ref[...]).astype(o_ref.dtype)


def _dilated_attention(qkv, slopes, bsz, seq, n_heads):
    t = qkv.shape[0]
    n_pairs = n_heads // 2
    tq = DILATED_PATTERNS[0][0] // DILATED_PATTERNS[0][1]
    shortest_class = seq // max(r for _, r in DILATED_PATTERNS)
    n_sub = min(DILATED_GROUP, shortest_class // tq)
    spec = lambda off: pl.BlockSpec((seq, PAIR), lambda b, p: (b, off + p))
    return pl.pallas_call(
        functools.partial(_dilated_kernel, seq=seq, tq=tq, n_sub=n_sub),
        out_shape=jax.ShapeDtypeStruct((t, n_heads * HEAD_DIM), BF16),
        grid=(bsz, n_pairs),
        in_specs=[pl.BlockSpec(memory_space=pltpu.SMEM), spec(0), spec(n_pairs), spec(2 * n_pairs)],
        out_specs=pl.BlockSpec((seq, PAIR), lambda b, p: (b, p)),
        scratch_shapes=[pltpu.VMEM((seq, PAIR), F32)] * 3,
        compiler_params=_cparams(("parallel", "parallel")),
        name="dilated_window_attention",
    )(slopes, qkv, qkv, qkv)


def _logf_cumsum_kernel(f_ref, b_ref, o_ref, floor_ref, carry_ref, low_ref, *, tb):
    i = pl.program_id(1)

    @pl.when(i == 0)
    def _():
        carry_ref[...] = jnp.zeros_like(carry_ref)
        low_ref[...] = jnp.zeros_like(low_ref)

    x = f_ref[...] + b_ref[...]
    log_f = jnp.minimum(x, 0.0) - jnp.log1p(jnp.exp(-jnp.abs(x)))
    row = lax.broadcasted_iota(jnp.int32, (tb, tb), 0)
    col = lax.broadcasted_iota(jnp.int32, (tb, tb), 1)
    lower = jnp.where(col <= row, 1.0, 0.0).astype(BF16)
    cum = _split_dot(lower, log_f, 3) + carry_ref[...]
    o_ref[...] = cum
    carry_ref[...] = cum[tb - 1:tb, :]
    low = jnp.minimum(low_ref[...], jnp.min(cum, axis=0, keepdims=True))
    low_ref[...] = low
    floor_ref[pl.ds(i, 1), :] = low


def _logf_cumsum(rest, b_f_row, bsz, seq, col_block, tb):
    t = rest.shape[0]
    nb = seq // tb
    return pl.pallas_call(
        functools.partial(_logf_cumsum_kernel, tb=tb),
        out_shape=(jax.ShapeDtypeStruct((t, LANES), F32),
                   jax.ShapeDtypeStruct((bsz, nb, LANES), F32)),
        grid=(bsz, nb),
        in_specs=[pl.BlockSpec((tb, LANES), lambda b, i: (b * nb + i, col_block)),
                  pl.BlockSpec((1, LANES), lambda b, i: (0, 0))],
        out_specs=(pl.BlockSpec((tb, LANES), lambda b, i: (b * nb + i, 0)),
                   pl.BlockSpec((None, nb, LANES), lambda b, i: (b, 0, 0))),
        scratch_shapes=[pltpu.VMEM((1, LANES), F32), pltpu.VMEM((1, LANES), F32)],
        compiler_params=_cparams(("parallel", "arbitrary")),
        name="log_forget_cumsum",
    )(rest, b_f_row)


def _fox_kernel(q_ref, k_ref, v_ref, fq_ref, fk_ref, floor_ref, o_ref, acc_ref, m_ref, knorm_ref,
                *, tq, seq):
    pair = pl.program_id(1)
    qi = pl.program_id(2)
    reps = tq // LANES
    first = _head_masks(tq)

    @pl.when(qi == 0)
    def _():
        def norms(c, carry):
            kk = k_ref[pl.ds(pl.multiple_of(c * tq, tq), tq), :].astype(F32)
            sq = kk * kk
            n0 = jnp.sum(jnp.where(first, sq, 0.0), axis=1, keepdims=True)
            n1 = jnp.sum(jnp.where(first, 0.0, sq), axis=1, keepdims=True)
            return jnp.maximum(carry[0], n0), jnp.maximum(carry[1], n1)

        zero = jnp.zeros((tq, 1), F32)
        n0, n1 = lax.fori_loop(0, seq // tq, norms, (zero, zero))
        knorm_ref[0] = jnp.sqrt(jnp.max(n0))
        knorm_ref[1] = jnp.sqrt(jnp.max(n1))

    q = q_ref[...] * jnp.asarray(HEAD_DIM ** -0.5, BF16)
    q_heads = (jnp.where(first, q, jnp.zeros_like(q)), jnp.where(first, jnp.zeros_like(q), q))
    fq_all = fq_ref[...]
    lane = lax.broadcasted_iota(jnp.int32, (tq, LANES), 1)
    f_q = [jnp.broadcast_to(jnp.sum(jnp.where(lane == 2 * pair + h, fq_all, 0.0), axis=1,
                                    keepdims=True), (tq, LANES)) for h in range(2)]
    row = lax.broadcasted_iota(jnp.int32, (tq, tq), 0)
    col = lax.broadcasted_iota(jnp.int32, (tq, tq), 1)
    causal = col <= row
    ones = jnp.ones((tq, PAIR), BF16)

    acc_ref[...] = jnp.zeros_like(acc_ref)
    m_ref[...] = jnp.full_like(m_ref, NEG_INF)

    def step(j, masked):
        k0 = pl.multiple_of(j * tq, tq)
        kb = k_ref[pl.ds(k0, tq), :]
        vb = v_ref[pl.ds(k0, tq), :]
        v_heads = (jnp.where(first, vb, ones), jnp.where(first, ones, vb))
        for h in range(2):
            f_k = fk_ref[pl.ds(2 * pair + h, 1), pl.ds(k0, tq)]
            s = lax.dot_general(q_heads[h], kb, (((1,), (1,)), ((), ())),
                                preferred_element_type=F32)
            s = s + jnp.tile(f_q[h], (1, reps)) - f_k
            if masked:
                s = jnp.where(causal, s, NEG_INF)
            m_old = m_ref[h]
            m_new = jnp.maximum(m_old, jnp.max(s, axis=-1, keepdims=True))
            alpha = jnp.exp(m_old - m_new)
            p = jnp.exp(s - jnp.tile(m_new, (1, reps)))
            acc_ref[h] = alpha * acc_ref[h] + jnp.dot(p.astype(BF16), v_heads[h],
                                                      preferred_element_type=F32)
            m_ref[h] = m_new

    step(qi, True)

    floors = floor_ref[...]
    blk = lax.broadcasted_iota(jnp.int32, floors.shape, 0)
    head_lane = lax.broadcasted_iota(jnp.int32, floors.shape, 1)
    skip = []
    for h in range(2):
        q32 = q_heads[h].astype(F32)
        q_norm = jnp.sqrt(jnp.sum(q32 * q32, axis=1, keepdims=True))
        reach = f_q[h] + q_norm * (knorm_ref[h] * FOX_NORM_SLACK) - m_ref[h]
        limit = jnp.max(reach) + FOX_LOG_UNDERFLOW
        dead = jnp.logical_and(jnp.logical_and(head_lane == 2 * pair + h, blk < qi), floors > limit)
        skip.append(jnp.sum(jnp.where(dead, 1, 0)))
    start = jnp.minimum(skip[0], skip[1])

    def body(j, _):
        step(j, False)
        return 0

    lax.fori_loop(start, qi, body, 0)
    outs = [acc_ref[h] / pltpu.roll(acc_ref[h], HEAD_DIM, 1) for h in range(2)]
    o_ref[...] = jnp.where(first, outs[0], outs[1]).astype(o_ref.dtype)


FOX_BLOCK = 512


def _forgetting_attention(qkv, f_cum, f_cum_t, f_floor, bsz, seq, n_heads):
    t = qkv.shape[0]
    tq = FOX_BLOCK
    n_pairs = n_heads // 2
    nq = seq // tq
    return pl.pallas_call(
        functools.partial(_fox_kernel, tq=tq, seq=seq),
        out_shape=jax.ShapeDtypeStruct((t, n_heads * HEAD_DIM), BF16),
        grid=(bsz, n_pairs, nq),
        in_specs=[pl.BlockSpec((tq, PAIR), lambda b, p, i: (b * nq + i, p)),
                  pl.BlockSpec((seq, PAIR), lambda b, p, i: (b, n_pairs + p)),
                  pl.BlockSpec((seq, PAIR), lambda b, p, i: (b, 2 * n_pairs + p)),
                  pl.BlockSpec((tq, LANES), lambda b, p, i: (b * nq + i, 0)),
                  pl.BlockSpec((None, n_heads, seq), lambda b, p, i: (b, 0, 0)),
                  pl.BlockSpec((None, nq, LANES), lambda b, p, i: (b, 0, 0))],
        out_specs=pl.BlockSpec((tq, PAIR), lambda b, p, i: (b * nq + i, p)),
        scratch_shapes=[pltpu.VMEM((2, tq, PAIR), F32), pltpu.VMEM((2, tq, LANES), F32),
                        pltpu.SMEM((2,), F32)],
        compiler_params=_cparams(("arbitrary", "arbitrary", "arbitrary")),
        name="forgetting_attention",
    )(qkv, qkv, qkv, f_cum, f_cum_t, f_floor)


CONV_TAIL = 8


def _ssd_kernel(*refs, n_heads, d_inner, n_xbc):
    xbc_refs = refs[:n_xbc]
    (z_ref, dt_ref, dtt_ref, bias_row_ref, bias_col_ref, alog_row_ref, alog_col_ref, convw_ref,
     convb_ref, dskip_ref, gate_ref, o_ref, state_ref, xc_ref, tail_ref) = refs[n_xbc:]
    q = SSD_CHUNK
    n = SSM_STATE
    hpg = n_heads // SSM_GROUPS
    gw = hpg * HEAD_DIM

    @pl.when(pl.program_id(1) == 0)
    def _():
        state_ref[...] = jnp.zeros_like(state_ref)
        tail_ref[...] = jnp.zeros_like(tail_ref)

    width = convw_ref.shape[0]
    x_raw = jnp.concatenate([r[...] for r in xbc_refs], axis=1)
    xx = jnp.concatenate([tail_ref[...], x_raw], axis=0)
    taps = convw_ref[...]
    y = convb_ref[...] + taps[width - 1:width, :] * x_raw
    for tap in range(width - 1):
        start = CONV_TAIL - (width - 1 - tap)
        y = y + taps[tap:tap + 1, :] * xx[start:start + q, :]
    xc_ref[...] = y * (1.0 / (1.0 + jnp.exp(-y)))
    tail_ref[...] = x_raw[q - CONV_TAIL:q, :]

    row = lax.broadcasted_iota(jnp.int32, (q, q), 0)
    col = lax.broadcasted_iota(jnp.int32, (q, q), 1)
    causal = col <= row
    lower = jnp.where(causal, 1.0, 0.0).astype(BF16)
    upper = jnp.where(row <= col, 1.0, 0.0).astype(BF16)
    er = lax.broadcasted_iota(jnp.int32, (LANES, d_inner), 0)
    ec = lax.broadcasted_iota(jnp.int32, (LANES, d_inner), 1)
    expand = jnp.where(ec // HEAD_DIM == er, 1.0, 0.0).astype(BF16)
    first = _head_masks(q)

    dt = _softplus(dt_ref[...] + bias_row_ref[...])
    a_cum = _split_dot(lower, dt * (-jnp.exp(alog_row_ref[...])), 3)
    a_last = a_cum[q - 1:q, :]
    dt_t = _softplus(dtt_ref[...] + bias_col_ref[...])
    a_cum_t = _dot_split(dt_t * (-jnp.exp(alog_col_ref[...])), upper, 3)

    dt_x = _dot_split(dt, expand, 2)
    decay_in_x = _dot_split(jnp.exp(a_cum), expand, 2)
    decay_out_x = _dot_split(jnp.exp(a_last - a_cum), expand, 2)
    chunk_decay_x = _dot_split(jnp.broadcast_to(jnp.exp(a_last), (8, LANES)), expand, 2)[0:1, :]

    xs = xc_ref[:, 0:d_inner]
    xdt = xs * dt_x
    xdt_bf = xdt.astype(BF16)
    xend_bf = (xdt * decay_out_x).astype(BF16)

    y_groups = []
    for g in range(SSM_GROUPS):
        b_g = xc_ref[:, d_inner + g * n:d_inner + (g + 1) * n].astype(BF16)
        c_g = xc_ref[:, d_inner + (SSM_GROUPS + g) * n:d_inner + (SSM_GROUPS + g + 1) * n].astype(BF16)
        cb = lax.dot_general(c_g, b_g, (((1,), (1,)), ((), ())), preferred_element_type=F32)
        state = state_ref[g]
        y_off = jnp.dot(c_g, state.astype(BF16), preferred_element_type=F32)
        y_g = y_off * decay_in_x[:, g * gw:(g + 1) * gw]
        diag_pairs = []
        for pr in range(hpg // 2):
            lo = g * gw + pr * PAIR
            x_pair = xdt_bf[:, lo:lo + PAIR]
            acc = None
            for hh in range(2):
                h = g * hpg + pr * 2 + hh
                seg = a_cum[:, h:h + 1] - a_cum_t[h:h + 1, :]
                m = (cb * jnp.where(causal, jnp.exp(seg), 0.0)).astype(BF16)
                x_h = jnp.where(first, x_pair, jnp.zeros_like(x_pair)) if hh == 0 else \
                    jnp.where(first, jnp.zeros_like(x_pair), x_pair)
                d = jnp.dot(m, x_h, preferred_element_type=F32)
                acc = d if acc is None else acc + d
            diag_pairs.append(acc)
        y_groups.append(y_g + jnp.concatenate(diag_pairs, axis=1))
        new_state = lax.dot_general(b_g, xend_bf[:, g * gw:(g + 1) * gw], (((0,), (0,)), ((), ())),
                                    preferred_element_type=F32)
        state_ref[g] = state * chunk_decay_x[:, g * gw:(g + 1) * gw] + new_state

    y = jnp.concatenate(y_groups, axis=1) + dskip_ref[...] * xs
    z = z_ref[...]
    gated = y * (z * (1.0 / (1.0 + jnp.exp(-z))))
    gn = d_inner // SSM_GROUPS
    normed = []
    for g in range(SSM_GROUPS):
        gg = gated[:, g * gn:(g + 1) * gn]
        normed.append(gg * lax.rsqrt(jnp.mean(gg * gg, axis=-1, keepdims=True) + NORM_EPS))
    o_ref[...] = (jnp.concatenate(normed, axis=1) * gate_ref[...]).astype(o_ref.dtype)


def _ssd(rest, dt_t, dt_bias, a_log, conv_w, conv_b, d_skip, gate_norm, bsz, seq, n_heads, d_inner,
         z_col, xbc_col, dt_col_block):
    t = rest.shape[0]
    nc = seq // SSD_CHUNK
    width, conv_ch = conv_w.shape
    pad = LANES - n_heads
    row = lambda v: jnp.pad(v, (0, pad)).reshape(1, LANES)
    colv = lambda v: v.reshape(n_heads, 1)
    zb = z_col // d_inner
    tc = math.gcd(xbc_col, conv_ch)
    n_xbc = conv_ch // tc
    const = lambda shape: pl.BlockSpec(shape, lambda b, c: (0,) * len(shape))
    chunk = lambda cols, blk: pl.BlockSpec((SSD_CHUNK, cols), lambda b, c: (b * nc + c, blk))
    return pl.pallas_call(
        functools.partial(_ssd_kernel, n_heads=n_heads, d_inner=d_inner, n_xbc=n_xbc),
        out_shape=jax.ShapeDtypeStruct((t, d_inner), BF16),
        grid=(bsz, nc),
        in_specs=[chunk(tc, xbc_col // tc + j) for j in range(n_xbc)] + [
            chunk(d_inner, zb), chunk(LANES, dt_col_block),
            pl.BlockSpec((None, n_heads, SSD_CHUNK), lambda b, c: (b, 0, c)),
            const((1, LANES)), const((n_heads, 1)), const((1, LANES)), const((n_heads, 1)),
            const((width, conv_ch)), const((1, conv_ch)), const((1, d_inner)), const((1, d_inner))],
        out_specs=pl.BlockSpec((SSD_CHUNK, d_inner), lambda b, c: (b * nc + c, 0)),
        scratch_shapes=[pltpu.VMEM((SSM_GROUPS, SSM_STATE, d_inner // SSM_GROUPS), F32),
                        pltpu.VMEM((SSD_CHUNK, conv_ch), F32), pltpu.VMEM((CONV_TAIL, conv_ch), F32)],
        compiler_params=_cparams(("parallel", "arbitrary")),
        name="ssd_chunk_scan",
    )(*([rest] * n_xbc), rest, rest, dt_t, row(dt_bias), colv(dt_bias), row(a_log), colv(a_log),
      conv_w, conv_b.reshape(1, conv_ch), jnp.repeat(d_skip, HEAD_DIM).reshape(1, d_inner),
      gate_norm.reshape(1, d_inner))


def _even_mixer(h, w_in, w_out, layer, post_w, resid, next_w, bsz, seq):
    d = h.shape[1]
    n_heads = d // (2 * HEAD_DIM)
    hw = n_heads * HEAD_DIM
    qkv_a = _matmul_f32_weight(h, w_in, layer, 0, 3 * hw, BF16)
    qkv_b = _matmul_f32_weight(h, w_in, layer, 3 * hw, 3 * hw, F32, tm=1024)
    oa = _stick_breaking(qkv_a, bsz, seq, n_heads, 0, hw, 2 * hw)
    slopes = jnp.exp2(-ALIBI_MAX_EXP * jnp.arange(1, n_heads + 1, dtype=F32) / n_heads)
    ob = _dilated_attention(qkv_b, slopes, bsz, seq, n_heads)
    return _matmul_norm_residual([oa, ob], w_out, layer, post_w, resid, next_w, tk=512)


def _odd_mixer(h, w_in_all, layer, b_f, conv_w, conv_b, dt_bias, a_log, d_skip, gate_norm, w_out,
               post_w, resid, next_w, bsz, seq):
    d = h.shape[1]
    n_heads = d // (2 * HEAD_DIM)
    hw = n_heads * HEAD_DIM
    d_inner = hw
    conv_ch = d_inner + 2 * SSM_GROUPS * SSM_STATE
    pad = LANES - n_heads
    c0 = 3 * hw
    w_in = w_in_all[layer]
    f_w = w_in[:, c0:c0 + n_heads]
    z_w = w_in[:, c0 + n_heads:c0 + n_heads + d_inner]
    xbc_w = w_in[:, c0 + n_heads + d_inner:c0 + n_heads + d_inner + conv_ch]
    dt_w = w_in[:, c0 + n_heads + d_inner + conv_ch:]
    w_rest = jnp.concatenate([z_w, xbc_w, jnp.pad(f_w, ((0, 0), (0, pad))),
                              jnp.pad(dt_w, ((0, 0), (0, pad)))], axis=1).astype(BF16)
    qkv = _matmul_f32_weight(h, w_in_all, layer, 0, c0, BF16)
    rest = _matmul(h, w_rest, F32)
    f_block = (d_inner + conv_ch) // LANES
    dt_block = f_block + 1

    f_cum, f_floor = _logf_cumsum(rest, jnp.pad(b_f, (0, pad)).reshape(1, LANES), bsz, seq, f_block,
                                  FOX_BLOCK)
    f_cum_t = f_cum[:, :n_heads].reshape(bsz, seq, n_heads).transpose(0, 2, 1)
    oc = _forgetting_attention(qkv, f_cum, f_cum_t, f_floor, bsz, seq, n_heads)

    dt_col = dt_block * LANES
    dt_t = rest[:, dt_col:dt_col + n_heads].reshape(bsz, seq, n_heads).transpose(0, 2, 1)
    y = _ssd(rest, dt_t, dt_bias, a_log, conv_w, conv_b, d_skip, gate_norm, bsz, seq, n_heads,
             d_inner, 0, d_inner, dt_block)
    return _matmul_norm_residual([oc, y], w_out, layer, post_w, resid, next_w, tk=512)


def _mlp(h, w_up, w_down, layer, post_w, resid, next_w):
    hidden = _matmul_f32_weight(h, w_up, layer, 0, w_up.shape[2], BF16, relu2=True)
    return _matmul_norm_residual([hidden], w_down, layer, post_w, resid, next_w)


def kernel(x, mix_norm_pre, mix_norm_post, mlp_norm_pre, mlp_norm_post, ab_w_in, ab_w_out,
           cd_w_in, cd_b_f, cd_conv_w, cd_conv_b, cd_dt_bias, cd_a_log, cd_d_skip,
           cd_gate_norm, cd_w_out, mlp_w_up, mlp_w_down):
    bsz, seq, d = x.shape
    depth = mix_norm_pre.shape[0]
    xf = x.reshape(bsz * seq, d)
    h = _rmsnorm(xf, mix_norm_pre[0])
    ab_w_out_bf, cd_w_out_bf, w_down_bf = (w.astype(BF16) for w in (ab_w_out, cd_w_out, mlp_w_down))
    for layer in range(depth):
        i = layer // 2
        if layer % 2 == 0:
            xf, h = _even_mixer(h, ab_w_in, ab_w_out_bf, i, mix_norm_post[layer], xf,
                                mlp_norm_pre[layer], bsz, seq)
        else:
            xf, h = _odd_mixer(h, cd_w_in, i, cd_b_f[i], cd_conv_w[i], cd_conv_b[i], cd_dt_bias[i],
                               cd_a_log[i], cd_d_skip[i], cd_gate_norm[i], cd_w_out_bf,
                               mix_norm_post[layer], xf, mlp_norm_pre[layer], bsz, seq)
        next_w = mix_norm_pre[layer + 1] if layer + 1 < depth else None
        out = _mlp(h, mlp_w_up, w_down_bf, layer, mlp_norm_post[layer], xf, next_w)
        xf, h = out if next_w is not None else (out, None)
    return xf.reshape(bsz, seq, d)
```

```python
import functools
import math

import jax
import jax.numpy as jnp
from jax import lax
from jax.experimental import pallas as pl
from jax.experimental.pallas import tpu as pltpu

F32 = jnp.float32
BF16 = jnp.bfloat16

LANES = 128
HEAD_DIM = 64
PAIR = 2 * HEAD_DIM
NORM_EPS = 1e-6
NEG_INF = -1e30
DILATED_PATTERNS = ((128, 1), (512, 4), (2048, 16))
DILATED_GROUP = 4
ALIBI_MAX_EXP = 8.0
SSM_GROUPS = 4
SSM_STATE = 128
SSD_CHUNK = 128
SB_LOG_UNDERFLOW = -104.0
FOX_LOG_UNDERFLOW = 104.0
FOX_NORM_SLACK = 1.001
VMEM_LIMIT_BYTES = 56 * 1024 * 1024


def _cparams(sem):
    return pltpu.CompilerParams(dimension_semantics=sem, vmem_limit_bytes=VMEM_LIMIT_BYTES)


def _softplus(x):
    return jnp.maximum(x, 0.0) + jnp.log1p(jnp.exp(-jnp.abs(x)))


def _split_bf16(x, parts):
    out = []
    rem = x
    for _ in range(parts):
        hi = rem.astype(BF16)
        out.append(hi)
        rem = rem - hi.astype(F32)
    return out


def _dot_split(x, m, parts):
    acc = None
    for t in _split_bf16(x, parts):
        d = jnp.dot(t, m, preferred_element_type=F32)
        acc = d if acc is None else acc + d
    return acc


def _split_dot(m, x, parts):
    acc = None
    for t in _split_bf16(x, parts):
        d = jnp.dot(m, t, preferred_element_type=F32)
        acc = d if acc is None else acc + d
    return acc


def _rmsnorm_kernel(x_ref, w_ref, o_ref):
    x = x_ref[...]
    y = x * lax.rsqrt(jnp.mean(x * x, axis=-1, keepdims=True) + NORM_EPS)
    o_ref[...] = (y * w_ref[...]).astype(o_ref.dtype)


def _rmsnorm(x2d, w, tm=512):
    t, d = x2d.shape
    return pl.pallas_call(
        _rmsnorm_kernel,
        out_shape=jax.ShapeDtypeStruct((t, d), BF16),
        grid=(t // tm,),
        in_specs=[pl.BlockSpec((tm, d), lambda i: (i, 0)),
                  pl.BlockSpec((1, d), lambda i: (0, 0))],
        out_specs=pl.BlockSpec((tm, d), lambda i: (i, 0)),
        compiler_params=_cparams(("parallel",)),
        name="rmsnorm",
    )(x2d, w.reshape(1, d))


def _mm_kernel(a_ref, b_ref, o_ref, *, relu2):
    acc = jnp.dot(a_ref[...], b_ref[...], preferred_element_type=F32)
    if relu2:
        acc = jnp.square(jnp.maximum(acc, 0.0))
    o_ref[...] = acc.astype(o_ref.dtype)


def _matmul(a, b, out_dtype, *, relu2=False, tm=1024, tn=1792):
    m, kk = a.shape
    n = b.shape[1]
    tm, tn = min(tm, m), min(tn, n)
    while n % tn:
        tn -= LANES
    return pl.pallas_call(
        functools.partial(_mm_kernel, relu2=relu2),
        out_shape=jax.ShapeDtypeStruct((m, n), out_dtype),
        grid=(m // tm, n // tn),
        in_specs=[pl.BlockSpec((tm, kk), lambda i, j: (i, 0)),
                  pl.BlockSpec((kk, tn), lambda i, j: (0, j))],
        out_specs=pl.BlockSpec((tm, tn), lambda i, j: (i, j)),
        compiler_params=_cparams(("parallel", "parallel")),
        name="matmul",
    )(a, b)


def _mm_f32w_kernel(a_ref, b_ref, o_ref, w_ref, *, relu2):
    @pl.when(pl.program_id(1) == 0)
    def _():
        w_ref[...] = b_ref[...].astype(BF16)

    acc = jnp.dot(a_ref[...], w_ref[...], preferred_element_type=F32)
    if relu2:
        acc = jnp.square(jnp.maximum(acc, 0.0))
    o_ref[...] = acc.astype(o_ref.dtype)


def _matmul_f32_weight(a, w, layer, col0, n, out_dtype, *, relu2=False, tm=2048, tn=1024):
    m, kk = a.shape
    tm = min(tm, m)
    g = math.gcd(n, col0)
    tn = max(t for t in range(LANES, min(tn, g) + 1, LANES) if g % t == 0)
    jb = col0 // tn
    return pl.pallas_call(
        functools.partial(_mm_f32w_kernel, relu2=relu2),
        out_shape=jax.ShapeDtypeStruct((m, n), out_dtype),
        grid=(n // tn, m // tm),
        in_specs=[pl.BlockSpec((tm, kk), lambda j, i: (i, 0)),
                  pl.BlockSpec((None, kk, tn), lambda j, i: (layer, 0, jb + j))],
        out_specs=pl.BlockSpec((tm, tn), lambda j, i: (i, j)),
        scratch_shapes=[pltpu.VMEM((kk, tn), BF16)],
        compiler_params=_cparams(("arbitrary", "arbitrary")),
        name="matmul_f32_weight",
    )(a, w)


def _mm_norm_res_kernel(*refs, steps, n_chunk, m_chunk, with_next):
    n_parts = len(steps)
    a_refs = refs[:n_parts]
    b_ref, w_ref, r_ref = refs[n_parts:n_parts + 3]
    rest = refs[n_parts + 3:]
    wn_ref, o_ref, h_ref = rest if with_next else (None, rest[0], None)
    k = pl.program_id(1)
    nk = sum(steps)
    tm, n = o_ref.shape

    @pl.when(k == 0)
    def _():
        o_ref[...] = jnp.zeros_like(o_ref)

    def accumulate(a_ref):
        a = a_ref[...]
        for n0 in range(0, n, n_chunk):
            o_ref[:, n0:n0 + n_chunk] += jnp.dot(a, b_ref[:, n0:n0 + n_chunk],
                                                 preferred_element_type=F32)

    lo = 0
    for a_ref, cnt in zip(a_refs, steps):
        if n_parts == 1:
            accumulate(a_ref)
        else:
            pl.when(jnp.logical_and(k >= lo, k < lo + cnt))(functools.partial(accumulate, a_ref))
        lo += cnt

    @pl.when(k == nk - 1)
    def _():
        for m0 in range(0, tm, m_chunk):
            rows = slice(m0, m0 + m_chunk)
            y = o_ref[rows, :]
            y = y * lax.rsqrt(jnp.mean(y * y, axis=-1, keepdims=True) + NORM_EPS)
            x_new = r_ref[rows, :] + y * w_ref[...]
            o_ref[rows, :] = x_new
            if with_next:
                hn = x_new * lax.rsqrt(jnp.mean(x_new * x_new, axis=-1, keepdims=True) + NORM_EPS)
                h_ref[rows, :] = (hn * wn_ref[...]).astype(h_ref.dtype)


def _matmul_norm_residual(a_parts, b, layer, w, resid, next_w=None, *, tm=1024, tk=1024):
    m = a_parts[0].shape[0]
    n = b.shape[2]
    tm = min(tm, m)
    tk = min([tk] + [a.shape[1] for a in a_parts])
    steps = tuple(a.shape[1] // tk for a in a_parts)
    starts = [sum(steps[:p]) for p in range(len(steps))]
    with_next = next_w is not None

    def a_spec(start, cnt):
        return pl.BlockSpec((tm, tk), lambda i, k: (i, jnp.clip(k - start, 0, cnt - 1)))

    row_spec = pl.BlockSpec((1, n), lambda i, k: (0, 0))
    tile_spec = pl.BlockSpec((tm, n), lambda i, k: (i, 0))
    in_specs = [a_spec(s, c) for s, c in zip(starts, steps)]
    in_specs += [pl.BlockSpec((None, tk, n), lambda i, k: (layer, k, 0)), row_spec, tile_spec]
    args = list(a_parts) + [b, w.reshape(1, n), resid]
    out_shape = jax.ShapeDtypeStruct((m, n), F32)
    out_specs = tile_spec
    if with_next:
        in_specs.append(row_spec)
        args.append(next_w.reshape(1, n))
        out_shape = (out_shape, jax.ShapeDtypeStruct((m, n), BF16))
        out_specs = (tile_spec, tile_spec)
    return pl.pallas_call(
        functools.partial(_mm_norm_res_kernel, steps=steps, n_chunk=min(n, 512),
                          m_chunk=min(tm, 256), with_next=with_next),
        out_shape=out_shape,
        grid=(m // tm, sum(steps)),
        in_specs=in_specs,
        out_specs=out_specs,
        compiler_params=_cparams(("parallel", "arbitrary")),
        name="matmul_norm_residual",
    )(*args)


def _head_masks(rows):
    lane = lax.broadcasted_iota(jnp.int32, (rows, PAIR), 1)
    return lane < HEAD_DIM


SB_BLOCK = 128
SB_LOOKBACK = 256
SB_GROUP = 4


def _sb_kernel(q_ref, k_ref, v_ref, upper_ref, o_ref, acc_ref, c_ref, *, seq):
    tq, back = SB_BLOCK, SB_LOOKBACK
    span = back + tq
    first = _head_masks(tq)
    row = lax.broadcasted_iota(jnp.int32, (tq, span), 0)
    col = lax.broadcasted_iota(jnp.int32, (tq, span), 1)
    scale = jnp.asarray(HEAD_DIM ** -0.5, BF16)

    def split_heads(q):
        return jnp.where(first, q, jnp.zeros_like(q)), jnp.where(first, jnp.zeros_like(q), q)

    def scores(qh, kb, visible=None):
        z = lax.dot_general(qh, kb, (((1,), (1,)), ((), ())), preferred_element_type=F32)
        if visible is not None:
            z = jnp.where(visible, z, NEG_INF)
        soft = jnp.log(1.0 + jnp.exp(-jnp.abs(z)))
        log_beta = jnp.minimum(z, 0.0) - soft
        return log_beta, log_beta - z

    def combine(acc0, acc1):
        return jnp.where(first, acc0, acc1).astype(o_ref.dtype)

    def group(t_block, k_block, layout):
        n_rows = max(k_off for k_off, _ in layout) + span
        k_base = pl.multiple_of(k_block * tq, tq)
        k_all = k_ref[pl.ds(k_base, n_rows), :]
        v_all = v_ref[pl.ds(k_base, n_rows), :]
        live = None
        for u, (k_off, lag) in enumerate(layout):
            rows = pl.ds(pl.multiple_of((t_block + u) * tq, tq), tq)
            q_heads = split_heads(q_ref[rows, :] * scale)
            strict = col < row + lag
            kb, vb = k_all[k_off:k_off + span], v_all[k_off:k_off + span]
            accs = []
            for h in range(2):
                log_beta, log_keep = scores(q_heads[h], kb, strict)
                later = jnp.dot(log_keep.astype(BF16), upper_ref[...], preferred_element_type=F32)
                w = jnp.exp(log_beta + later)
                accs.append(jnp.dot(w.astype(BF16), vb, preferred_element_type=F32))
                c = jnp.sum(log_keep, axis=1, keepdims=True)
                acc_ref[u, h] = accs[h]
                c_ref[u, h] = c
                live = c if live is None else jnp.maximum(live, c)
            o_ref[rows, :] = combine(accs[0], accs[1])

        @pl.when(jnp.max(live) > SB_LOG_UNDERFLOW)
        def _():
            for u, (k_off, _) in enumerate(layout):
                rows = pl.ds(pl.multiple_of((t_block + u) * tq, tq), tq)
                q_heads = split_heads(q_ref[rows, :] * scale)

                def cond(carry):
                    j, alive = carry
                    return jnp.logical_and(j >= 0, alive)

                def walk(carry):
                    j, _ = carry
                    key_rows = pl.ds(pl.multiple_of(j * tq, tq), tq)
                    kb, vb = k_ref[key_rows, :], v_ref[key_rows, :]
                    for h in range(2):
                        log_beta, log_keep = scores(q_heads[h], kb)
                        later = jnp.dot(log_keep.astype(BF16), upper_ref[0:tq, 0:tq],
                                        preferred_element_type=F32)
                        c = c_ref[u, h]
                        w = jnp.exp(log_beta + later + c)
                        acc_ref[u, h] += jnp.dot(w.astype(BF16), vb, preferred_element_type=F32)
                        c_ref[u, h] = c + jnp.sum(log_keep, axis=1, keepdims=True)
                    return j - 1, jnp.max(c_ref[u]) > SB_LOG_UNDERFLOW

                lax.while_loop(cond, walk, (k_block + k_off // tq - 1,
                                            jnp.max(c_ref[u]) > SB_LOG_UNDERFLOW))
                o_ref[rows, :] = combine(acc_ref[u, 0], acc_ref[u, 1])

    back_blocks = back // tq
    head_layout = [(max(u - back_blocks, 0) * tq, min(u, back_blocks) * tq) for u in range(SB_GROUP)]
    group(0, 0, head_layout)

    def body(n, _):
        group(n * SB_GROUP, n * SB_GROUP - back_blocks, [(u * tq, back) for u in range(SB_GROUP)])
        return 0

    lax.fori_loop(1, seq // (SB_GROUP * tq), body, 0)


def _stick_breaking(proj, bsz, seq, n_heads, q_col, k_col, v_col):
    t = proj.shape[0]
    n_pairs = n_heads // 2
    qb, kb, vb = q_col // PAIR, k_col // PAIR, v_col // PAIR
    span = SB_LOOKBACK + SB_BLOCK
    upper = (jnp.arange(span)[:, None] > jnp.arange(span)[None, :]).astype(BF16)
    seq_spec = lambda off: pl.BlockSpec((seq, PAIR), lambda b, p: (b, off + p))
    return pl.pallas_call(
        functools.partial(_sb_kernel, seq=seq),
        out_shape=jax.ShapeDtypeStruct((t, n_heads * HEAD_DIM), BF16),
        grid=(bsz, n_pairs),
        in_specs=[seq_spec(qb), seq_spec(kb), seq_spec(vb),
                  pl.BlockSpec((span, span), lambda b, p: (0, 0))],
        out_specs=pl.BlockSpec((seq, PAIR), lambda b, p: (b, p)),
        scratch_shapes=[pltpu.VMEM((SB_GROUP, 2, SB_BLOCK, PAIR), F32),
                        pltpu.VMEM((SB_GROUP, 2, SB_BLOCK, 1), F32)],
        compiler_params=_cparams(("parallel", "parallel")),
        name="stick_breaking_attention",
    )(proj, proj, proj, upper)


def _dilated_kernel(slope_ref, q_ref, k_ref, v_ref, o_ref, num_ref, den_ref, max_ref, *, seq, tq,
                    n_sub):
    pair = pl.program_id(1)
    first = _head_masks(tq)
    span = 2 * tq
    group = n_sub * tq
    first_keys = _head_masks(group + tq)
    ones = jnp.ones((group + tq, PAIR), BF16)
    row = lax.broadcasted_iota(jnp.int32, (tq, span), 0)
    col = lax.broadcasted_iota(jnp.int32, (tq, span), 1)
    scale = HEAD_DIM ** -0.5

    for branch, (window, r) in enumerate(DILATED_PATTERNS):
        assert window // r == tq
        per_class = seq // (r * group)
        stride = None if r == 1 else r

        def bias(h, class_start):
            hops = tq + row - col
            valid = jnp.logical_and(hops >= 0, hops <= tq)
            if class_start:
                valid = jnp.logical_and(valid, col >= tq)
            return jnp.where(valid, -(slope_ref[2 * pair + h] * (hops * r).astype(F32)), NEG_INF)

        bias_rest = [bias(h, False) for h in range(2)]
        bias_head = [bias(h, True) for h in range(2)]

        def body(n, _):
            cls = n // per_class
            i = n - cls * per_class
            m0 = i * group
            rows = pl.ds(cls + r * m0, group, stride=stride)
            before = pl.ds(cls + r * jnp.maximum(m0 - tq, 0), tq, stride=stride)
            q_all = (q_ref[rows, :] * scale).astype(BF16)
            k_all = jnp.concatenate([k_ref[before, :], k_ref[rows, :]], axis=0).astype(BF16)
            v_all = jnp.concatenate([v_ref[before, :], v_ref[rows, :]], axis=0).astype(BF16)
            v_heads = (jnp.where(first_keys, v_all, ones), jnp.where(first_keys, ones, v_all))
            nums, dens, maxes = [], [], []
            for u in range(n_sub):
                q = q_all[u * tq:(u + 1) * tq]
                kc = k_all[u * tq:u * tq + span]
                q_heads = (jnp.where(first, q, jnp.zeros_like(q)),
                           jnp.where(first, jnp.zeros_like(q), q))
                res, mxs = [], []
                for h in range(2):
                    z = lax.dot_general(q_heads[h], kc, (((1,), (1,)), ((), ())),
                                        preferred_element_type=F32)
                    b = bias_rest[h] if u else jnp.where(i == 0, bias_head[h], bias_rest[h])
                    logits = z + b
                    m = jnp.max(logits, axis=-1, keepdims=True)
                    p = jnp.exp(logits - m)
                    res.append(jnp.dot(p.astype(BF16), v_heads[h][u * tq:u * tq + span],
                                       preferred_element_type=F32))
                    mxs.append(jnp.broadcast_to(m, (tq, PAIR)))
                nums.append(jnp.where(first, res[0], res[1]))
                dens.append(pltpu.roll(jnp.where(first, res[1], res[0]), HEAD_DIM, 1))
                maxes.append(jnp.where(first, mxs[0], mxs[1]))
            num = jnp.concatenate(nums, axis=0)
            den = jnp.concatenate(dens, axis=0)
            mx = jnp.concatenate(maxes, axis=0)
            if branch == 0:
                num_ref[rows, :] = num
                den_ref[rows, :] = den
                max_ref[rows, :] = mx
            else:
                m_old = max_ref[rows, :]
                m_new = jnp.maximum(m_old, mx)
                w_old = jnp.exp(m_old - m_new)
                w_new = jnp.exp(mx - m_new)
                num_ref[rows, :] = w_old * num_ref[rows, :] + w_new * num
                den_ref[rows, :] = w_old * den_ref[rows, :] + w_new * den
                max_ref[rows, :] = m_new
            return 0

        lax.fori_loop(0, seq // group, body, 0, unroll=2)

    o_ref[...] = (num_ref[...] / den_ref[...]).astype(o_ref.dtype)


def _dilated_attention(qkv, slopes, bsz, seq, n_heads):
    t = qkv.shape[0]
    n_pairs = n_heads // 2
    tq = DILATED_PATTERNS[0][0] // DILATED_PATTERNS[0][1]
    shortest_class = seq // max(r for _, r in DILATED_PATTERNS)
    n_sub = min(DILATED_GROUP, shortest_class // tq)
    spec = lambda off: pl.BlockSpec((seq, PAIR), lambda b, p: (b, off + p))
    return pl.pallas_call(
        functools.partial(_dilated_kernel, seq=seq, tq=tq, n_sub=n_sub),
        out_shape=jax.ShapeDtypeStruct((t, n_heads * HEAD_DIM), BF16),
        grid=(bsz, n_pairs),
        in_specs=[pl.BlockSpec(memory_space=pltpu.SMEM), spec(0), spec(n_pairs), spec(2 * n_pairs)],
        out_specs=pl.BlockSpec((seq, PAIR), lambda b, p: (b, p)),
        scratch_shapes=[pltpu.VMEM((seq, PAIR), F32)] * 3,
        compiler_params=_cparams(("parallel", "parallel")),
        name="dilated_window_attention",
    )(slopes, qkv, qkv, qkv)


def _logf_cumsum_kernel(f_ref, b_ref, o_ref, floor_ref, carry_ref, low_ref, *, tb):
    i = pl.program_id(1)

    @pl.when(i == 0)
    def _():
        carry_ref[...] = jnp.zeros_like(carry_ref)
        low_ref[...] = jnp.zeros_like(low_ref)

    x = f_ref[...] + b_ref[...]
    log_f = jnp.minimum(x, 0.0) - jnp.log1p(jnp.exp(-jnp.abs(x)))
    row = lax.broadcasted_iota(jnp.int32, (tb, tb), 0)
    col = lax.broadcasted_iota(jnp.int32, (tb, tb), 1)
    lower = jnp.where(col <= row, 1.0, 0.0).astype(BF16)
    cum = _split_dot(lower, log_f, 3) + carry_ref[...]
    o_ref[...] = cum
    carry_ref[...] = cum[tb - 1:tb, :]
    low = jnp.minimum(low_ref[...], jnp.min(cum, axis=0, keepdims=True))
    low_ref[...] = low
    floor_ref[pl.ds(i, 1), :] = low


def _logf_cumsum(rest, b_f_row, bsz, seq, col_block, tb):
    t = rest.shape[0]
    nb = seq // tb
    return pl.pallas_call(
        functools.partial(_logf_cumsum_kernel, tb=tb),
        out_shape=(jax.ShapeDtypeStruct((t, LANES), F32),
                   jax.ShapeDtypeStruct((bsz, nb, LANES), F32)),
        grid=(bsz, nb),
        in_specs=[pl.BlockSpec((tb, LANES), lambda b, i: (b * nb + i, col_block)),
                  pl.BlockSpec((1, LANES), lambda b, i: (0, 0))],
        out_specs=(pl.BlockSpec((tb, LANES), lambda b, i: (b * nb + i, 0)),
                   pl.BlockSpec((None, nb, LANES), lambda b, i: (b, 0, 0))),
        scratch_shapes=[pltpu.VMEM((1, LANES), F32), pltpu.VMEM((1, LANES), F32)],
        compiler_params=_cparams(("parallel", "arbitrary")),
        name="log_forget_cumsum",
    )(rest, b_f_row)


FOX_BLOCK = 512
FOX_F_PARTS = 3


def _fox_kernel(q_ref, k_ref, v_ref, f_ref, floor_ref, o_ref, acc_ref, m_ref, ka_ref, knorm_ref,
                *, tq, seq):
    pair = pl.program_id(1)
    qi = pl.program_id(2)
    reps = tq // LANES
    first = _head_masks(tq)
    lane = lax.broadcasted_iota(jnp.int32, (tq, PAIR), 1)
    f_base = (HEAD_DIM, 0)
    f_lanes = [jnp.logical_and(lane >= f_base[h], lane < f_base[h] + FOX_F_PARTS) for h in range(2)]

    @pl.when(qi == 0)
    def _():
        src = lax.broadcasted_iota(jnp.int32, (LANES, PAIR), 0)
        dst = lax.broadcasted_iota(jnp.int32, (LANES, PAIR), 1)
        place = [[jnp.where(jnp.logical_and(src == 2 * pair + h, dst == f_base[h] + t), 1.0, 0.0)
                  .astype(BF16) for t in range(FOX_F_PARTS)] for h in range(2)]

        def prepare(c, carry):
            rows = pl.ds(pl.multiple_of(c * tq, tq), tq)
            kb = k_ref[rows, :]
            parts = _split_bf16(f_ref[rows, :], FOX_F_PARTS)
            for h in range(2):
                f_terms = None
                for t in range(FOX_F_PARTS):
                    d = jnp.dot(parts[t], place[h][t], preferred_element_type=F32)
                    f_terms = d if f_terms is None else f_terms + d
                own = first if h == 0 else jnp.logical_not(first)
                ka_ref[h, rows, :] = jnp.where(own, kb, (-f_terms).astype(BF16))
            sq = kb.astype(F32)
            sq = sq * sq
            n0 = jnp.sum(jnp.where(first, sq, 0.0), axis=1, keepdims=True)
            n1 = jnp.sum(jnp.where(first, 0.0, sq), axis=1, keepdims=True)
            return jnp.maximum(carry[0], n0), jnp.maximum(carry[1], n1)

        zero = jnp.zeros((tq, 1), F32)
        n0, n1 = lax.fori_loop(0, seq // tq, prepare, (zero, zero))
        knorm_ref[0] = jnp.sqrt(jnp.max(n0))
        knorm_ref[1] = jnp.sqrt(jnp.max(n1))

    q = q_ref[...] * jnp.asarray(HEAD_DIM ** -0.5, BF16)
    zeros = jnp.zeros_like(q)
    q_heads = (jnp.where(first, q, zeros), jnp.where(first, zeros, q))
    one = jnp.ones_like(q)
    q_aug = [jnp.where(f_lanes[h], one, q_heads[h]) for h in range(2)]
    row = lax.broadcasted_iota(jnp.int32, (tq, tq), 0)
    col = lax.broadcasted_iota(jnp.int32, (tq, tq), 1)
    causal = col <= row
    ones = jnp.ones((tq, PAIR), BF16)

    acc_ref[...] = jnp.zeros_like(acc_ref)
    m_ref[...] = jnp.full_like(m_ref, NEG_INF)

    def step(j, masked):
        rows = pl.ds(pl.multiple_of(j * tq, tq), tq)
        vb = v_ref[rows, :]
        v_heads = (jnp.where(first, vb, ones), jnp.where(first, ones, vb))
        for h in range(2):
            s = lax.dot_general(q_aug[h], ka_ref[h, rows, :], (((1,), (1,)), ((), ())),
                                preferred_element_type=F32)
            if masked:
                s = jnp.where(causal, s, NEG_INF)
            m_old = m_ref[h]
            m_new = jnp.maximum(m_old, jnp.max(s, axis=-1, keepdims=True))
            alpha = jnp.exp(m_old - m_new)
            p = jnp.exp(s - jnp.tile(m_new, (1, reps)))
            acc_ref[h] = alpha * acc_ref[h] + jnp.dot(p.astype(BF16), v_heads[h],
                                                      preferred_element_type=F32)
            m_ref[h] = m_new

    step(qi, True)

    floors = floor_ref[...]
    blk = lax.broadcasted_iota(jnp.int32, floors.shape, 0)
    head_lane = lax.broadcasted_iota(jnp.int32, floors.shape, 1)
    skip = []
    for h in range(2):
        q32 = q_heads[h].astype(F32)
        q_norm = jnp.sqrt(jnp.sum(q32 * q32, axis=1, keepdims=True))
        reach = q_norm * (knorm_ref[h] * FOX_NORM_SLACK) - m_ref[h]
        limit = jnp.max(reach) + FOX_LOG_UNDERFLOW
        dead = jnp.logical_and(jnp.logical_and(head_lane == 2 * pair + h, blk < qi), floors > limit)
        skip.append(jnp.sum(jnp.where(dead, 1, 0)))
    start = jnp.minimum(skip[0], skip[1])

    def body(j, _):
        step(j, False)
        return 0

    lax.fori_loop(start, qi, body, 0)
    outs = [acc_ref[h] / pltpu.roll(acc_ref[h], HEAD_DIM, 1) for h in range(2)]
    o_ref[...] = jnp.where(first, outs[0], outs[1]).astype(o_ref.dtype)


def _forgetting_attention(qkv, f_cum, f_floor, bsz, seq, n_heads):
    t = qkv.shape[0]
    tq = FOX_BLOCK
    n_pairs = n_heads // 2
    nq = seq // tq
    return pl.pallas_call(
        functools.partial(_fox_kernel, tq=tq, seq=seq),
        out_shape=jax.ShapeDtypeStruct((t, n_heads * HEAD_DIM), BF16),
        grid=(bsz, n_pairs, nq),
        in_specs=[pl.BlockSpec((tq, PAIR), lambda b, p, i: (b * nq + i, p)),
                  pl.BlockSpec((seq, PAIR), lambda b, p, i: (b, n_pairs + p)),
                  pl.BlockSpec((seq, PAIR), lambda b, p, i: (b, 2 * n_pairs + p)),
                  pl.BlockSpec((seq, LANES), lambda b, p, i: (b, 0)),
                  pl.BlockSpec((None, nq, LANES), lambda b, p, i: (b, 0, 0))],
        out_specs=pl.BlockSpec((tq, PAIR), lambda b, p, i: (b * nq + i, p)),
        scratch_shapes=[pltpu.VMEM((2, tq, PAIR), F32), pltpu.VMEM((2, tq, LANES), F32),
                        pltpu.VMEM((2, seq, PAIR), BF16), pltpu.SMEM((2,), F32)],
        compiler_params=_cparams(("arbitrary", "arbitrary", "arbitrary")),
        name="forgetting_attention",
    )(qkv, qkv, qkv, f_cum, f_floor)


CONV_TAIL = 8


def _ssd_kernel(*refs, n_heads, d_inner, n_xbc):
    xbc_refs = refs[:n_xbc]
    (z_ref, dt_ref, dtt_ref, bias_row_ref, bias_col_ref, alog_row_ref, alog_col_ref, convw_ref,
     convb_ref, dskip_ref, gate_ref, o_ref, state_ref, xc_ref, tail_ref) = refs[n_xbc:]
    q = SSD_CHUNK
    n = SSM_STATE
    hpg = n_heads // SSM_GROUPS
    gw = hpg * HEAD_DIM

    @pl.when(pl.program_id(1) == 0)
    def _():
        state_ref[...] = jnp.zeros_like(state_ref)
        tail_ref[...] = jnp.zeros_like(tail_ref)

    width = convw_ref.shape[0]
    x_raw = jnp.concatenate([r[...] for r in xbc_refs], axis=1)
    xx = jnp.concatenate([tail_ref[...], x_raw], axis=0)
    taps = convw_ref[...]
    y = convb_ref[...] + taps[width - 1:width, :] * x_raw
    for tap in range(width - 1):
        start = CONV_TAIL - (width - 1 - tap)
        y = y + taps[tap:tap + 1, :] * xx[start:start + q, :]
    xc_ref[...] = y * (1.0 / (1.0 + jnp.exp(-y)))
    tail_ref[...] = x_raw[q - CONV_TAIL:q, :]

    row = lax.broadcasted_iota(jnp.int32, (q, q), 0)
    col = lax.broadcasted_iota(jnp.int32, (q, q), 1)
    causal = col <= row
    lower = jnp.where(causal, 1.0, 0.0).astype(BF16)
    upper = jnp.where(row <= col, 1.0, 0.0).astype(BF16)
    er = lax.broadcasted_iota(jnp.int32, (LANES, d_inner), 0)
    ec = lax.broadcasted_iota(jnp.int32, (LANES, d_inner), 1)
    expand = jnp.where(ec // HEAD_DIM == er, 1.0, 0.0).astype(BF16)
    first = _head_masks(q)

    dt = _softplus(dt_ref[...] + bias_row_ref[...])
    a_cum = _split_dot(lower, dt * (-jnp.exp(alog_row_ref[...])), 3)
    a_last = a_cum[q - 1:q, :]
    dt_t = _softplus(dtt_ref[...] + bias_col_ref[...])
    a_cum_t = _dot_split(dt_t * (-jnp.exp(alog_col_ref[...])), upper, 3)

    dt_x = _dot_split(dt, expand, 2)
    decay_in_x = _dot_split(jnp.exp(a_cum), expand, 2)
    decay_out_x = _dot_split(jnp.exp(a_last - a_cum), expand, 2)
    chunk_decay_x = _dot_split(jnp.broadcast_to(jnp.exp(a_last), (8, LANES)), expand, 2)[0:1, :]

    xs = xc_ref[:, 0:d_inner]
    xdt = xs * dt_x
    xdt_bf = xdt.astype(BF16)
    xend_bf = (xdt * decay_out_x).astype(BF16)

    y_groups = []
    for g in range(SSM_GROUPS):
        b_g = xc_ref[:, d_inner + g * n:d_inner + (g + 1) * n].astype(BF16)
        c_g = xc_ref[:, d_inner + (SSM_GROUPS + g) * n:d_inner + (SSM_GROUPS + g + 1) * n].astype(BF16)
        cb = lax.dot_general(c_g, b_g, (((1,), (1,)), ((), ())), preferred_element_type=F32)
        state = state_ref[g]
        y_off = jnp.dot(c_g, state.astype(BF16), preferred_element_type=F32)
        y_g = y_off * decay_in_x[:, g * gw:(g + 1) * gw]
        diag_pairs = []
        for pr in range(hpg // 2):
            lo = g * gw + pr * PAIR
            x_pair = xdt_bf[:, lo:lo + PAIR]
            acc = None
            for hh in range(2):
                h = g * hpg + pr * 2 + hh
                seg = a_cum[:, h:h + 1] - a_cum_t[h:h + 1, :]
                m = (cb * jnp.where(causal, jnp.exp(seg), 0.0)).astype(BF16)
                x_h = jnp.where(first, x_pair, jnp.zeros_like(x_pair)) if hh == 0 else \
                    jnp.where(first, jnp.zeros_like(x_pair), x_pair)
                d = jnp.dot(m, x_h, preferred_element_type=F32)
                acc = d if acc is None else acc + d
            diag_pairs.append(acc)
        y_groups.append(y_g + jnp.concatenate(diag_pairs, axis=1))
        new_state = lax.dot_general(b_g, xend_bf[:, g * gw:(g + 1) * gw], (((0,), (0,)), ((), ())),
                                    preferred_element_type=F32)
        state_ref[g] = state * chunk_decay_x[:, g * gw:(g + 1) * gw] + new_state

    y = jnp.concatenate(y_groups, axis=1) + dskip_ref[...] * xs
    z = z_ref[...]
    gated = y * (z * (1.0 / (1.0 + jnp.exp(-z))))
    gn = d_inner // SSM_GROUPS
    normed = []
    for g in range(SSM_GROUPS):
        gg = gated[:, g * gn:(g + 1) * gn]
        normed.append(gg * lax.rsqrt(jnp.mean(gg * gg, axis=-1, keepdims=True) + NORM_EPS))
    o_ref[...] = (jnp.concatenate(normed, axis=1) * gate_ref[...]).astype(o_ref.dtype)


def _ssd(rest, dt_t, dt_bias, a_log, conv_w, conv_b, d_skip, gate_norm, bsz, seq, n_heads, d_inner,
         z_col, xbc_col, dt_col_block):
    t = rest.shape[0]
    nc = seq // SSD_CHUNK
    width, conv_ch = conv_w.shape
    pad = LANES - n_heads
    row = lambda v: jnp.pad(v, (0, pad)).reshape(1, LANES)
    colv = lambda v: v.reshape(n_heads, 1)
    zb = z_col // d_inner
    tc = math.gcd(xbc_col, conv_ch)
    n_xbc = conv_ch // tc
    const = lambda shape: pl.BlockSpec(shape, lambda b, c: (0,) * len(shape))
    chunk = lambda cols, blk: pl.BlockSpec((SSD_CHUNK, cols), lambda b, c: (b * nc + c, blk))
    return pl.pallas_call(
        functools.partial(_ssd_kernel, n_heads=n_heads, d_inner=d_inner, n_xbc=n_xbc),
        out_shape=jax.ShapeDtypeStruct((t, d_inner), BF16),
        grid=(bsz, nc),
        in_specs=[chunk(tc, xbc_col // tc + j) for j in range(n_xbc)] + [
            chunk(d_inner, zb), chunk(LANES, dt_col_block),
            pl.BlockSpec((None, n_heads, SSD_CHUNK), lambda b, c: (b, 0, c)),
            const((1, LANES)), const((n_heads, 1)), const((1, LANES)), const((n_heads, 1)),
            const((width, conv_ch)), const((1, conv_ch)), const((1, d_inner)), const((1, d_inner))],
        out_specs=pl.BlockSpec((SSD_CHUNK, d_inner), lambda b, c: (b * nc + c, 0)),
        scratch_shapes=[pltpu.VMEM((SSM_GROUPS, SSM_STATE, d_inner // SSM_GROUPS), F32),
                        pltpu.VMEM((SSD_CHUNK, conv_ch), F32), pltpu.VMEM((CONV_TAIL, conv_ch), F32)],
        compiler_params=_cparams(("parallel", "arbitrary")),
        name="ssd_chunk_scan",
    )(*([rest] * n_xbc), rest, rest, dt_t, row(dt_bias), colv(dt_bias), row(a_log), colv(a_log),
      conv_w, conv_b.reshape(1, conv_ch), jnp.repeat(d_skip, HEAD_DIM).reshape(1, d_inner),
      gate_norm.reshape(1, d_inner))


def _even_mixer(h, w_in, w_out, layer, post_w, resid, next_w, bsz, seq):
    d = h.shape[1]
    n_heads = d // (2 * HEAD_DIM)
    hw = n_heads * HEAD_DIM
    qkv_a = _matmul_f32_weight(h, w_in, layer, 0, 3 * hw, BF16)
    qkv_b = _matmul_f32_weight(h, w_in, layer, 3 * hw, 3 * hw, F32, tm=1024)
    oa = _stick_breaking(qkv_a, bsz, seq, n_heads, 0, hw, 2 * hw)
    slopes = jnp.exp2(-ALIBI_MAX_EXP * jnp.arange(1, n_heads + 1, dtype=F32) / n_heads)
    ob = _dilated_attention(qkv_b, slopes, bsz, seq, n_heads)
    return _matmul_norm_residual([oa, ob], w_out, layer, post_w, resid, next_w, tk=512)


def _odd_mixer(h, w_in_all, layer, b_f, conv_w, conv_b, dt_bias, a_log, d_skip, gate_norm, w_out,
               post_w, resid, next_w, bsz, seq):
    d = h.shape[1]
    n_heads = d // (2 * HEAD_DIM)
    hw = n_heads * HEAD_DIM
    d_inner = hw
    conv_ch = d_inner + 2 * SSM_GROUPS * SSM_STATE
    pad = LANES - n_heads
    c0 = 3 * hw
    w_in = w_in_all[layer]
    f_w = w_in[:, c0:c0 + n_heads]
    z_w = w_in[:, c0 + n_heads:c0 + n_heads + d_inner]
    xbc_w = w_in[:, c0 + n_heads + d_inner:c0 + n_heads + d_inner + conv_ch]
    dt_w = w_in[:, c0 + n_heads + d_inner + conv_ch:]
    w_rest = jnp.concatenate([z_w, xbc_w, jnp.pad(f_w, ((0, 0), (0, pad))),
                              jnp.pad(dt_w, ((0, 0), (0, pad)))], axis=1).astype(BF16)
    qkv = _matmul(h, w_in[:, :c0].astype(BF16), BF16)
    rest = _matmul(h, w_rest, F32)
    f_block = (d_inner + conv_ch) // LANES
    dt_block = f_block + 1

    f_cum, f_floor = _logf_cumsum(rest, jnp.pad(b_f, (0, pad)).reshape(1, LANES), bsz, seq, f_block,
                                  FOX_BLOCK)
    oc = _forgetting_attention(qkv, f_cum, f_floor, bsz, seq, n_heads)

    dt_col = dt_block * LANES
    dt_t = rest[:, dt_col:dt_col + n_heads].reshape(bsz, seq, n_heads).transpose(0, 2, 1)
    y = _ssd(rest, dt_t, dt_bias, a_log, conv_w, conv_b, d_skip, gate_norm, bsz, seq, n_heads,
             d_inner, 0, d_inner, dt_block)
    return _matmul_norm_residual([oc, y], w_out, layer, post_w, resid, next_w, tk=512)


def _mlp(h, w_up, w_down, layer, post_w, resid, next_w):
    hidden = _matmul_f32_weight(h, w_up, layer, 0, w_up.shape[2], BF16, relu2=True)
    return _matmul_norm_residual([hidden], w_down, layer, post_w, resid, next_w)


def kernel(x, mix_norm_pre, mix_norm_post, mlp_norm_pre, mlp_norm_post, ab_w_in, ab_w_out,
           cd_w_in, cd_b_f, cd_conv_w, cd_conv_b, cd_dt_bias, cd_a_log, cd_d_skip,
           cd_gate_norm, cd_w_out, mlp_w_up, mlp_w_down):
    bsz, seq, d = x.shape
    depth = mix_norm_pre.shape[0]
    xf = x.reshape(bsz * seq, d)
    h = _rmsnorm(xf, mix_norm_pre[0])
    ab_w_out_bf, cd_w_out_bf, w_down_bf = (w.astype(BF16) for w in (ab_w_out, cd_w_out, mlp_w_down))
    for layer in range(depth):
        i = layer // 2
        if layer % 2 == 0:
            xf, h = _even_mixer(h, ab_w_in, ab_w_out_bf, i, mix_norm_post[layer], xf,
                                mlp_norm_pre[layer], bsz, seq)
        else:
            xf, h = _odd_mixer(h, cd_w_in, i, cd_b_f[i], cd_conv_w[i], cd_conv_b[i], cd_dt_bias[i],
                               cd_a_log[i], cd_d_skip[i], cd_gate_norm[i], cd_w_out_bf,
                               mix_norm_post[layer], xf, mlp_norm_pre[layer], bsz, seq)
        next_w = mix_norm_pre[layer + 1] if layer + 1 < depth else None
        out = _mlp(h, mlp_w_up, w_down_bf, layer, mlp_norm_post[layer], xf, next_w)
        xf, h = out if next_w is not None else (out, None)
    return xf.reshape(bsz, seq, d)
```

```python
import functools
import math

import jax
import jax.numpy as jnp
from jax import lax
from jax.experimental import pallas as pl
from jax.experimental.pallas import tpu as pltpu

F32 = jnp.float32
BF16 = jnp.bfloat16

LANES = 128
HEAD_DIM = 64
PAIR = 2 * HEAD_DIM
NORM_EPS = 1e-6
NEG_INF = -1e30
DILATED_PATTERNS = ((128, 1), (512, 4), (2048, 16))
DILATED_GROUP = 4
ALIBI_MAX_EXP = 8.0
SSM_GROUPS = 4
SSM_STATE = 128
SSD_CHUNK = 128
SB_LOG_UNDERFLOW = -104.0
FOX_LOG_UNDERFLOW = 104.0
FOX_NORM_SLACK = 1.001
VMEM_LIMIT_BYTES = 56 * 1024 * 1024


def _cparams(sem):
    return pltpu.CompilerParams(dimension_semantics=sem, vmem_limit_bytes=VMEM_LIMIT_BYTES)


def _softplus(x):
    return jnp.maximum(x, 0.0) + jnp.log1p(jnp.exp(-jnp.abs(x)))


def _split_bf16(x, parts):
    out = []
    rem = x
    for _ in range(parts):
        hi = rem.astype(BF16)
        out.append(hi)
        rem = rem - hi.astype(F32)
    return out


def _dot_split(x, m, parts):
    acc = None
    for t in _split_bf16(x, parts):
        d = jnp.dot(t, m, preferred_element_type=F32)
        acc = d if acc is None else acc + d
    return acc


def _split_dot(m, x, parts):
    acc = None
    for t in _split_bf16(x, parts):
        d = jnp.dot(m, t, preferred_element_type=F32)
        acc = d if acc is None else acc + d
    return acc


def _rmsnorm_kernel(x_ref, w_ref, o_ref):
    x = x_ref[...]
    y = x * lax.rsqrt(jnp.mean(x * x, axis=-1, keepdims=True) + NORM_EPS)
    o_ref[...] = (y * w_ref[...]).astype(o_ref.dtype)


def _rmsnorm(x2d, w, tm=512):
    t, d = x2d.shape
    return pl.pallas_call(
        _rmsnorm_kernel,
        out_shape=jax.ShapeDtypeStruct((t, d), BF16),
        grid=(t // tm,),
        in_specs=[pl.BlockSpec((tm, d), lambda i: (i, 0)),
                  pl.BlockSpec((1, d), lambda i: (0, 0))],
        out_specs=pl.BlockSpec((tm, d), lambda i: (i, 0)),
        compiler_params=_cparams(("parallel",)),
        name="rmsnorm",
    )(x2d, w.reshape(1, d))


def _mm_kernel(a_ref, b_ref, o_ref, *, relu2):
    acc = jnp.dot(a_ref[...], b_ref[...], preferred_element_type=F32)
    if relu2:
        acc = jnp.square(jnp.maximum(acc, 0.0))
    o_ref[...] = acc.astype(o_ref.dtype)


def _matmul(a, b, out_dtype, *, relu2=False, tm=1024, tn=1792):
    m, kk = a.shape
    n = b.shape[1]
    tm, tn = min(tm, m), min(tn, n)
    while n % tn:
        tn -= LANES
    return pl.pallas_call(
        functools.partial(_mm_kernel, relu2=relu2),
        out_shape=jax.ShapeDtypeStruct((m, n), out_dtype),
        grid=(m // tm, n // tn),
        in_specs=[pl.BlockSpec((tm, kk), lambda i, j: (i, 0)),
                  pl.BlockSpec((kk, tn), lambda i, j: (0, j))],
        out_specs=pl.BlockSpec((tm, tn), lambda i, j: (i, j)),
        compiler_params=_cparams(("parallel", "parallel")),
        name="matmul",
    )(a, b)


def _mm_f32w_kernel(a_ref, b_ref, o_ref, w_ref, *, relu2):
    @pl.when(pl.program_id(1) == 0)
    def _():
        w_ref[...] = b_ref[...].astype(BF16)

    acc = jnp.dot(a_ref[...], w_ref[...], preferred_element_type=F32)
    if relu2:
        acc = jnp.square(jnp.maximum(acc, 0.0))
    o_ref[...] = acc.astype(o_ref.dtype)


def _matmul_f32_weight(a, w, layer, col0, n, out_dtype, *, relu2=False, tm=2048, tn=1024):
    m, kk = a.shape
    tm = min(tm, m)
    g = math.gcd(n, col0)
    tn = max(t for t in range(LANES, min(tn, g) + 1, LANES) if g % t == 0)
    jb = col0 // tn
    return pl.pallas_call(
        functools.partial(_mm_f32w_kernel, relu2=relu2),
        out_shape=jax.ShapeDtypeStruct((m, n), out_dtype),
        grid=(n // tn, m // tm),
        in_specs=[pl.BlockSpec((tm, kk), lambda j, i: (i, 0)),
                  pl.BlockSpec((None, kk, tn), lambda j, i: (layer, 0, jb + j))],
        out_specs=pl.BlockSpec((tm, tn), lambda j, i: (i, j)),
        scratch_shapes=[pltpu.VMEM((kk, tn), BF16)],
        compiler_params=_cparams(("arbitrary", "arbitrary")),
        name="matmul_f32_weight",
    )(a, w)


def _mm_norm_res_kernel(*refs, steps, n_chunk, m_chunk, with_next):
    n_parts = len(steps)
    a_refs = refs[:n_parts]
    b_ref, w_ref, r_ref = refs[n_parts:n_parts + 3]
    rest = refs[n_parts + 3:]
    wn_ref, o_ref, h_ref = rest if with_next else (None, rest[0], None)
    k = pl.program_id(1)
    nk = sum(steps)
    tm, n = o_ref.shape

    @pl.when(k == 0)
    def _():
        o_ref[...] = jnp.zeros_like(o_ref)

    def accumulate(a_ref):
        a = a_ref[...]
        for n0 in range(0, n, n_chunk):
            o_ref[:, n0:n0 + n_chunk] += jnp.dot(a, b_ref[:, n0:n0 + n_chunk],
                                                 preferred_element_type=F32)

    lo = 0
    for a_ref, cnt in zip(a_refs, steps):
        if n_parts == 1:
            accumulate(a_ref)
        else:
            pl.when(jnp.logical_and(k >= lo, k < lo + cnt))(functools.partial(accumulate, a_ref))
        lo += cnt

    @pl.when(k == nk - 1)
    def _():
        for m0 in range(0, tm, m_chunk):
            rows = slice(m0, m0 + m_chunk)
            y = o_ref[rows, :]
            y = y * lax.rsqrt(jnp.mean(y * y, axis=-1, keepdims=True) + NORM_EPS)
            x_new = r_ref[rows, :] + y * w_ref[...]
            o_ref[rows, :] = x_new
            if with_next:
                hn = x_new * lax.rsqrt(jnp.mean(x_new * x_new, axis=-1, keepdims=True) + NORM_EPS)
                h_ref[rows, :] = (hn * wn_ref[...]).astype(h_ref.dtype)


def _matmul_norm_residual(a_parts, b, layer, w, resid, next_w=None, *, tm=1024, tk=1024):
    m = a_parts[0].shape[0]
    n = b.shape[2]
    tm = min(tm, m)
    tk = min([tk] + [a.shape[1] for a in a_parts])
    steps = tuple(a.shape[1] // tk for a in a_parts)
    starts = [sum(steps[:p]) for p in range(len(steps))]
    with_next = next_w is not None

    def a_spec(start, cnt):
        return pl.BlockSpec((tm, tk), lambda i, k: (i, jnp.clip(k - start, 0, cnt - 1)))

    row_spec = pl.BlockSpec((1, n), lambda i, k: (0, 0))
    tile_spec = pl.BlockSpec((tm, n), lambda i, k: (i, 0))
    in_specs = [a_spec(s, c) for s, c in zip(starts, steps)]
    in_specs += [pl.BlockSpec((None, tk, n), lambda i, k: (layer, k, 0)), row_spec, tile_spec]
    args = list(a_parts) + [b, w.reshape(1, n), resid]
    out_shape = jax.ShapeDtypeStruct((m, n), F32)
    out_specs = tile_spec
    if with_next:
        in_specs.append(row_spec)
        args.append(next_w.reshape(1, n))
        out_shape = (out_shape, jax.ShapeDtypeStruct((m, n), BF16))
        out_specs = (tile_spec, tile_spec)
    return pl.pallas_call(
        functools.partial(_mm_norm_res_kernel, steps=steps, n_chunk=min(n, 512),
                          m_chunk=min(tm, 256), with_next=with_next),
        out_shape=out_shape,
        grid=(m // tm, sum(steps)),
        in_specs=in_specs,
        out_specs=out_specs,
        compiler_params=_cparams(("parallel", "arbitrary")),
        name="matmul_norm_residual",
    )(*args)


def _head_masks(rows):
    lane = lax.broadcasted_iota(jnp.int32, (rows, PAIR), 1)
    return lane < HEAD_DIM


SB_BLOCK = 128
SB_LOOKBACK = 256
SB_GROUP = 4


def _sb_kernel(q_ref, k_ref, v_ref, upper_ref, o_ref, acc_ref, c_ref, *, seq):
    tq, back = SB_BLOCK, SB_LOOKBACK
    span = back + tq
    first = _head_masks(tq)
    row = lax.broadcasted_iota(jnp.int32, (tq, span), 0)
    col = lax.broadcasted_iota(jnp.int32, (tq, span), 1)
    scale = jnp.asarray(HEAD_DIM ** -0.5, BF16)

    def split_heads(q):
        return jnp.where(first, q, jnp.zeros_like(q)), jnp.where(first, jnp.zeros_like(q), q)

    def scores(qh, kb, visible=None):
        z = lax.dot_general(qh, kb, (((1,), (1,)), ((), ())), preferred_element_type=F32)
        if visible is not None:
            z = jnp.where(visible, z, NEG_INF)
        soft = jnp.log(1.0 + jnp.exp(-jnp.abs(z)))
        log_beta = jnp.minimum(z, 0.0) - soft
        return log_beta, log_beta - z

    def combine(acc0, acc1):
        return jnp.where(first, acc0, acc1).astype(o_ref.dtype)

    def group(t_block, k_block, layout):
        n_rows = max(k_off for k_off, _ in layout) + span
        k_base = pl.multiple_of(k_block * tq, tq)
        k_all = k_ref[pl.ds(k_base, n_rows), :]
        v_all = v_ref[pl.ds(k_base, n_rows), :]
        live = None
        for u, (k_off, lag) in enumerate(layout):
            rows = pl.ds(pl.multiple_of((t_block + u) * tq, tq), tq)
            q_heads = split_heads(q_ref[rows, :] * scale)
            strict = col < row + lag
            kb, vb = k_all[k_off:k_off + span], v_all[k_off:k_off + span]
            accs = []
            for h in range(2):
                log_beta, log_keep = scores(q_heads[h], kb, strict)
                later = jnp.dot(log_keep.astype(BF16), upper_ref[...], preferred_element_type=F32)
                w = jnp.exp(log_beta + later)
                accs.append(jnp.dot(w.astype(BF16), vb, preferred_element_type=F32))
                c = jnp.sum(log_keep, axis=1, keepdims=True)
                acc_ref[u, h] = accs[h]
                c_ref[u, h] = c
                live = c if live is None else jnp.maximum(live, c)
            o_ref[rows, :] = combine(accs[0], accs[1])

        @pl.when(jnp.max(live) > SB_LOG_UNDERFLOW)
        def _():
            for u, (k_off, _) in enumerate(layout):
                rows = pl.ds(pl.multiple_of((t_block + u) * tq, tq), tq)
                q_heads = split_heads(q_ref[rows, :] * scale)

                def cond(carry):
                    j, alive = carry
                    return jnp.logical_and(j >= 0, alive)

                def walk(carry):
                    j, _ = carry
                    key_rows = pl.ds(pl.multiple_of(j * tq, tq), tq)
                    kb, vb = k_ref[key_rows, :], v_ref[key_rows, :]
                    for h in range(2):
                        log_beta, log_keep = scores(q_heads[h], kb)
                        later = jnp.dot(log_keep.astype(BF16), upper_ref[0:tq, 0:tq],
                                        preferred_element_type=F32)
                        c = c_ref[u, h]
                        w = jnp.exp(log_beta + later + c)
                        acc_ref[u, h] += jnp.dot(w.astype(BF16), vb, preferred_element_type=F32)
                        c_ref[u, h] = c + jnp.sum(log_keep, axis=1, keepdims=True)
                    return j - 1, jnp.max(c_ref[u]) > SB_LOG_UNDERFLOW

                lax.while_loop(cond, walk, (k_block + k_off // tq - 1,
                                            jnp.max(c_ref[u]) > SB_LOG_UNDERFLOW))
                o_ref[rows, :] = combine(acc_ref[u, 0], acc_ref[u, 1])

    back_blocks = back // tq
    head_layout = [(max(u - back_blocks, 0) * tq, min(u, back_blocks) * tq) for u in range(SB_GROUP)]
    group(0, 0, head_layout)

    def body(n, _):
        group(n * SB_GROUP, n * SB_GROUP - back_blocks, [(u * tq, back) for u in range(SB_GROUP)])
        return 0

    lax.fori_loop(1, seq // (SB_GROUP * tq), body, 0)


def _stick_breaking(proj, bsz, seq, n_heads, q_col, k_col, v_col):
    t = proj.shape[0]
    n_pairs = n_heads // 2
    qb, kb, vb = q_col // PAIR, k_col // PAIR, v_col // PAIR
    span = SB_LOOKBACK + SB_BLOCK
    upper = (jnp.arange(span)[:, None] > jnp.arange(span)[None, :]).astype(BF16)
    seq_spec = lambda off: pl.BlockSpec((seq, PAIR), lambda b, p: (b, off + p))
    return pl.pallas_call(
        functools.partial(_sb_kernel, seq=seq),
        out_shape=jax.ShapeDtypeStruct((t, n_heads * HEAD_DIM), BF16),
        grid=(bsz, n_pairs),
        in_specs=[seq_spec(qb), seq_spec(kb), seq_spec(vb),
                  pl.BlockSpec((span, span), lambda b, p: (0, 0))],
        out_specs=pl.BlockSpec((seq, PAIR), lambda b, p: (b, p)),
        scratch_shapes=[pltpu.VMEM((SB_GROUP, 2, SB_BLOCK, PAIR), F32),
                        pltpu.VMEM((SB_GROUP, 2, SB_BLOCK, 1), F32)],
        compiler_params=_cparams(("parallel", "parallel")),
        name="stick_breaking_attention",
    )(proj, proj, proj, upper)


def _dilated_kernel(slope_ref, q_ref, k_ref, v_ref, o_ref, num_ref, den_ref, max_ref, *, seq, tq,
                    n_sub):
    pair = pl.program_id(1)
    first = _head_masks(tq)
    span = 2 * tq
    group = n_sub * tq
    first_keys = _head_masks(group + tq)
    ones = jnp.ones((group + tq, PAIR), BF16)
    row = lax.broadcasted_iota(jnp.int32, (tq, span), 0)
    col = lax.broadcasted_iota(jnp.int32, (tq, span), 1)
    scale = HEAD_DIM ** -0.5

    for branch, (window, r) in enumerate(DILATED_PATTERNS):
        assert window // r == tq
        per_class = seq // (r * group)
        stride = None if r == 1 else r

        def bias(h, class_start):
            hops = tq + row - col
            valid = jnp.logical_and(hops >= 0, hops <= tq)
            if class_start:
                valid = jnp.logical_and(valid, col >= tq)
            return jnp.where(valid, -(slope_ref[2 * pair + h] * (hops * r).astype(F32)), NEG_INF)

        bias_rest = [bias(h, False) for h in range(2)]
        bias_head = [bias(h, True) for h in range(2)]

        def body(n, _):
            cls = n // per_class
            i = n - cls * per_class
            m0 = i * group
            rows = pl.ds(cls + r * m0, group, stride=stride)
            before = pl.ds(cls + r * jnp.maximum(m0 - tq, 0), tq, stride=stride)
            q_all = (q_ref[rows, :] * scale).astype(BF16)
            k_all = jnp.concatenate([k_ref[before, :], k_ref[rows, :]], axis=0).astype(BF16)
            v_all = jnp.concatenate([v_ref[before, :], v_ref[rows, :]], axis=0).astype(BF16)
            v_heads = (jnp.where(first_keys, v_all, ones), jnp.where(first_keys, ones, v_all))
            nums, dens, maxes = [], [], []
            for u in range(n_sub):
                q = q_all[u * tq:(u + 1) * tq]
                kc = k_all[u * tq:u * tq + span]
                q_heads = (jnp.where(first, q, jnp.zeros_like(q)),
                           jnp.where(first, jnp.zeros_like(q), q))
                res, mxs = [], []
                for h in range(2):
                    z = lax.dot_general(q_heads[h], kc, (((1,), (1,)), ((), ())),
                                        preferred_element_type=F32)
                    b = bias_rest[h] if u else jnp.where(i == 0, bias_head[h], bias_rest[h])
                    logits = z + b
                    m = jnp.max(logits, axis=-1, keepdims=True)
                    p = jnp.exp(logits - m)
                    res.append(jnp.dot(p.astype(BF16), v_heads[h][u * tq:u * tq + span],
                                       preferred_element_type=F32))
                    mxs.append(jnp.broadcast_to(m, (tq, PAIR)))
                nums.append(jnp.where(first, res[0], res[1]))
                dens.append(pltpu.roll(jnp.where(first, res[1], res[0]), HEAD_DIM, 1))
                maxes.append(jnp.where(first, mxs[0], mxs[1]))
            num = jnp.concatenate(nums, axis=0)
            den = jnp.concatenate(dens, axis=0)
            mx = jnp.concatenate(maxes, axis=0)
            if branch == 0:
                num_ref[rows, :] = num
                den_ref[rows, :] = den
                max_ref[rows, :] = mx
            else:
                m_old = max_ref[rows, :]
                m_new = jnp.maximum(m_old, mx)
                w_old = jnp.exp(m_old - m_new)
                w_new = jnp.exp(mx - m_new)
                num_ref[rows, :] = w_old * num_ref[rows, :] + w_new * num
                den_ref[rows, :] = w_old * den_ref[rows, :] + w_new * den
                max_ref[rows, :] = m_new
            return 0

        lax.fori_loop(0, seq // group, body, 0, unroll=2)

    o_ref[...] = (num_ref[...] / den_ref[...]).astype(o_ref.dtype)


def _dilated_attention(qkv, slopes, bsz, seq, n_heads):
    t = qkv.shape[0]
    n_pairs = n_heads // 2
    tq = DILATED_PATTERNS[0][0] // DILATED_PATTERNS[0][1]
    shortest_class = seq // max(r for _, r in DILATED_PATTERNS)
    n_sub = min(DILATED_GROUP, shortest_class // tq)
    spec = lambda off: pl.BlockSpec((seq, PAIR), lambda b, p: (b, off + p))
    return pl.pallas_call(
        functools.partial(_dilated_kernel, seq=seq, tq=tq, n_sub=n_sub),
        out_shape=jax.ShapeDtypeStruct((t, n_heads * HEAD_DIM), BF16),
        grid=(bsz, n_pairs),
        in_specs=[pl.BlockSpec(memory_space=pltpu.SMEM), spec(0), spec(n_pairs), spec(2 * n_pairs)],
        out_specs=pl.BlockSpec((seq, PAIR), lambda b, p: (b, p)),
        scratch_shapes=[pltpu.VMEM((seq, PAIR), F32)] * 3,
        compiler_params=_cparams(("parallel", "parallel")),
        name="dilated_window_attention",
    )(slopes, qkv, qkv, qkv)


def _logf_cumsum_kernel(f_ref, b_ref, o_ref, floor_ref, carry_ref, low_ref, *, tb):
    i = pl.program_id(1)

    @pl.when(i == 0)
    def _():
        carry_ref[...] = jnp.zeros_like(carry_ref)
        low_ref[...] = jnp.zeros_like(low_ref)

    x = f_ref[...] + b_ref[...]
    log_f = jnp.minimum(x, 0.0) - jnp.log1p(jnp.exp(-jnp.abs(x)))
    row = lax.broadcasted_iota(jnp.int32, (tb, tb), 0)
    col = lax.broadcasted_iota(jnp.int32, (tb, tb), 1)
    lower = jnp.where(col <= row, 1.0, 0.0).astype(BF16)
    cum = _split_dot(lower, log_f, 3) + carry_ref[...]
    o_ref[...] = cum
    carry_ref[...] = cum[tb - 1:tb, :]
    low = jnp.minimum(low_ref[...], jnp.min(cum, axis=0, keepdims=True))
    low_ref[...] = low
    floor_ref[pl.ds(i, 1), :] = low


def _logf_cumsum(rest, b_f_row, bsz, seq, col_block, tb):
    t = rest.shape[0]
    nb = seq // tb
    return pl.pallas_call(
        functools.partial(_logf_cumsum_kernel, tb=tb),
        out_shape=(jax.ShapeDtypeStruct((t, LANES), F32),
                   jax.ShapeDtypeStruct((bsz, nb, LANES), F32)),
        grid=(bsz, nb),
        in_specs=[pl.BlockSpec((tb, LANES), lambda b, i: (b * nb + i, col_block)),
                  pl.BlockSpec((1, LANES), lambda b, i: (0, 0))],
        out_specs=(pl.BlockSpec((tb, LANES), lambda b, i: (b * nb + i, 0)),
                   pl.BlockSpec((None, nb, LANES), lambda b, i: (b, 0, 0))),
        scratch_shapes=[pltpu.VMEM((1, LANES), F32), pltpu.VMEM((1, LANES), F32)],
        compiler_params=_cparams(("parallel", "arbitrary")),
        name="log_forget_cumsum",
    )(rest, b_f_row)


FOX_BLOCK = 1024
FOX_F_PARTS = 3


def _fox_kernel(q_ref, k_ref, v_ref, f_ref, floor_ref, o_ref, acc_ref, m_ref, ka_ref, knorm_ref,
                *, tq, seq):
    pair = pl.program_id(1)
    qi = pl.program_id(2)
    reps = tq // LANES
    first = _head_masks(tq)
    lane = lax.broadcasted_iota(jnp.int32, (tq, PAIR), 1)
    f_base = (HEAD_DIM, 0)
    f_lanes = [jnp.logical_and(lane >= f_base[h], lane < f_base[h] + FOX_F_PARTS) for h in range(2)]

    @pl.when(qi == 0)
    def _():
        src = lax.broadcasted_iota(jnp.int32, (LANES, PAIR), 0)
        dst = lax.broadcasted_iota(jnp.int32, (LANES, PAIR), 1)
        place = [[jnp.where(jnp.logical_and(src == 2 * pair + h, dst == f_base[h] + t), 1.0, 0.0)
                  .astype(BF16) for t in range(FOX_F_PARTS)] for h in range(2)]

        def prepare(c, carry):
            rows = pl.ds(pl.multiple_of(c * tq, tq), tq)
            kb = k_ref[rows, :]
            parts = _split_bf16(f_ref[rows, :], FOX_F_PARTS)
            for h in range(2):
                f_terms = None
                for t in range(FOX_F_PARTS):
                    d = jnp.dot(parts[t], place[h][t], preferred_element_type=F32)
                    f_terms = d if f_terms is None else f_terms + d
                own = first if h == 0 else jnp.logical_not(first)
                ka_ref[h, rows, :] = jnp.where(own, kb, (-f_terms).astype(BF16))
            sq = kb.astype(F32)
            sq = sq * sq
            n0 = jnp.sum(jnp.where(first, sq, 0.0), axis=1, keepdims=True)
            n1 = jnp.sum(jnp.where(first, 0.0, sq), axis=1, keepdims=True)
            return jnp.maximum(carry[0], n0), jnp.maximum(carry[1], n1)

        zero = jnp.zeros((tq, 1), F32)
        n0, n1 = lax.fori_loop(0, seq // tq, prepare, (zero, zero))
        knorm_ref[0] = jnp.sqrt(jnp.max(n0))
        knorm_ref[1] = jnp.sqrt(jnp.max(n1))

    q = q_ref[...] * jnp.asarray(HEAD_DIM ** -0.5, BF16)
    zeros = jnp.zeros_like(q)
    q_heads = (jnp.where(first, q, zeros), jnp.where(first, zeros, q))
    one = jnp.ones_like(q)
    q_aug = [jnp.where(f_lanes[h], one, q_heads[h]) for h in range(2)]
    row = lax.broadcasted_iota(jnp.int32, (tq, tq), 0)
    col = lax.broadcasted_iota(jnp.int32, (tq, tq), 1)
    causal = col <= row
    ones = jnp.ones((tq, PAIR), BF16)

    acc_ref[...] = jnp.zeros_like(acc_ref)
    m_ref[...] = jnp.full_like(m_ref, NEG_INF)

    def step(j, masked):
        rows = pl.ds(pl.multiple_of(j * tq, tq), tq)
        vb = v_ref[rows, :]
        v_heads = (jnp.where(first, vb, ones), jnp.where(first, ones, vb))
        for h in range(2):
            s = lax.dot_general(q_aug[h], ka_ref[h, rows, :], (((1,), (1,)), ((), ())),
                                preferred_element_type=F32)
            if masked:
                s = jnp.where(causal, s, NEG_INF)
            m_old = m_ref[h]
            m_new = jnp.maximum(m_old, jnp.max(s, axis=-1, keepdims=True))
            alpha = jnp.exp(m_old - m_new)
            p = jnp.exp(s - jnp.tile(m_new, (1, reps)))
            acc_ref[h] = alpha * acc_ref[h] + jnp.dot(p.astype(BF16), v_heads[h],
                                                      preferred_element_type=F32)
            m_ref[h] = m_new

    step(qi, True)

    floors = floor_ref[...]
    blk = lax.broadcasted_iota(jnp.int32, floors.shape, 0)
    head_lane = lax.broadcasted_iota(jnp.int32, floors.shape, 1)
    skip = []
    for h in range(2):
        q32 = q_heads[h].astype(F32)
        q_norm = jnp.sqrt(jnp.sum(q32 * q32, axis=1, keepdims=True))
        reach = q_norm * (knorm_ref[h] * FOX_NORM_SLACK) - m_ref[h]
        limit = jnp.max(reach) + FOX_LOG_UNDERFLOW
        dead = jnp.logical_and(jnp.logical_and(head_lane == 2 * pair + h, blk < qi), floors > limit)
        skip.append(jnp.sum(jnp.where(dead, 1, 0)))
    start = jnp.minimum(skip[0], skip[1])

    def body(j, _):
        step(j, False)
        return 0

    lax.fori_loop(start, qi, body, 0)
    outs = [acc_ref[h] / pltpu.roll(acc_ref[h], HEAD_DIM, 1) for h in range(2)]
    o_ref[...] = jnp.where(first, outs[0], outs[1]).astype(o_ref.dtype)


def _forgetting_attention(qkv, f_cum, f_floor, bsz, seq, n_heads):
    t = qkv.shape[0]
    tq = FOX_BLOCK
    n_pairs = n_heads // 2
    nq = seq // tq
    return pl.pallas_call(
        functools.partial(_fox_kernel, tq=tq, seq=seq),
        out_shape=jax.ShapeDtypeStruct((t, n_heads * HEAD_DIM), BF16),
        grid=(bsz, n_pairs, nq),
        in_specs=[pl.BlockSpec((tq, PAIR), lambda b, p, i: (b * nq + i, p)),
                  pl.BlockSpec((seq, PAIR), lambda b, p, i: (b, n_pairs + p)),
                  pl.BlockSpec((seq, PAIR), lambda b, p, i: (b, 2 * n_pairs + p)),
                  pl.BlockSpec((seq, LANES), lambda b, p, i: (b, 0)),
                  pl.BlockSpec((None, nq, LANES), lambda b, p, i: (b, 0, 0))],
        out_specs=pl.BlockSpec((tq, PAIR), lambda b, p, i: (b * nq + i, p)),
        scratch_shapes=[pltpu.VMEM((2, tq, PAIR), F32), pltpu.VMEM((2, tq, LANES), F32),
                        pltpu.VMEM((2, seq, PAIR), BF16), pltpu.SMEM((2,), F32)],
        compiler_params=_cparams(("arbitrary", "arbitrary", "arbitrary")),
        name="forgetting_attention",
    )(qkv, qkv, qkv, f_cum, f_floor)


CONV_TAIL = 8


def _ssd_kernel(*refs, n_heads, d_inner, n_xbc):
    xbc_refs = refs[:n_xbc]
    (z_ref, dt_ref, dtt_ref, bias_row_ref, bias_col_ref, alog_row_ref, alog_col_ref, convw_ref,
     convb_ref, dskip_ref, gate_ref, o_ref, state_ref, xc_ref, tail_ref) = refs[n_xbc:]
    q = SSD_CHUNK
    n = SSM_STATE
    hpg = n_heads // SSM_GROUPS
    gw = hpg * HEAD_DIM

    @pl.when(pl.program_id(1) == 0)
    def _():
        state_ref[...] = jnp.zeros_like(state_ref)
        tail_ref[...] = jnp.zeros_like(tail_ref)

    width = convw_ref.shape[0]
    x_raw = jnp.concatenate([r[...] for r in xbc_refs], axis=1)
    xx = jnp.concatenate([tail_ref[...], x_raw], axis=0)
    taps = convw_ref[...]
    y = convb_ref[...] + taps[width - 1:width, :] * x_raw
    for tap in range(width - 1):
        start = CONV_TAIL - (width - 1 - tap)
        y = y + taps[tap:tap + 1, :] * xx[start:start + q, :]
    xc_ref[...] = y * (1.0 / (1.0 + jnp.exp(-y)))
    tail_ref[...] = x_raw[q - CONV_TAIL:q, :]

    row = lax.broadcasted_iota(jnp.int32, (q, q), 0)
    col = lax.broadcasted_iota(jnp.int32, (q, q), 1)
    causal = col <= row
    lower = jnp.where(causal, 1.0, 0.0).astype(BF16)
    upper = jnp.where(row <= col, 1.0, 0.0).astype(BF16)
    er = lax.broadcasted_iota(jnp.int32, (LANES, d_inner), 0)
    ec = lax.broadcasted_iota(jnp.int32, (LANES, d_inner), 1)
    expand = jnp.where(ec // HEAD_DIM == er, 1.0, 0.0).astype(BF16)
    first = _head_masks(q)

    dt = _softplus(dt_ref[...] + bias_row_ref[...])
    a_cum = _split_dot(lower, dt * (-jnp.exp(alog_row_ref[...])), 3)
    a_last = a_cum[q - 1:q, :]
    dt_t = _softplus(dtt_ref[...] + bias_col_ref[...])
    a_cum_t = _dot_split(dt_t * (-jnp.exp(alog_col_ref[...])), upper, 3)

    dt_x = _dot_split(dt, expand, 2)
    decay_in_x = _dot_split(jnp.exp(a_cum), expand, 2)
    decay_out_x = _dot_split(jnp.exp(a_last - a_cum), expand, 2)
    chunk_decay_x = _dot_split(jnp.broadcast_to(jnp.exp(a_last), (8, LANES)), expand, 2)[0:1, :]

    xs = xc_ref[:, 0:d_inner]
    xdt = xs * dt_x
    xdt_bf = xdt.astype(BF16)
    xend_bf = (xdt * decay_out_x).astype(BF16)

    y_groups = []
    for g in range(SSM_GROUPS):
        b_g = xc_ref[:, d_inner + g * n:d_inner + (g + 1) * n].astype(BF16)
        c_g = xc_ref[:, d_inner + (SSM_GROUPS + g) * n:d_inner + (SSM_GROUPS + g + 1) * n].astype(BF16)
        cb = lax.dot_general(c_g, b_g, (((1,), (1,)), ((), ())), preferred_element_type=F32)
        state = state_ref[g]
        y_off = jnp.dot(c_g, state.astype(BF16), preferred_element_type=F32)
        y_g = y_off * decay_in_x[:, g * gw:(g + 1) * gw]
        diag_pairs = []
        for pr in range(hpg // 2):
            lo = g * gw + pr * PAIR
            x_pair = xdt_bf[:, lo:lo + PAIR]
            acc = None
            for hh in range(2):
                h = g * hpg + pr * 2 + hh
                seg = a_cum[:, h:h + 1] - a_cum_t[h:h + 1, :]
                m = (cb * jnp.where(causal, jnp.exp(seg), 0.0)).astype(BF16)
                x_h = jnp.where(first, x_pair, jnp.zeros_like(x_pair)) if hh == 0 else \
                    jnp.where(first, jnp.zeros_like(x_pair), x_pair)
                d = jnp.dot(m, x_h, preferred_element_type=F32)
                acc = d if acc is None else acc + d
            diag_pairs.append(acc)
        y_groups.append(y_g + jnp.concatenate(diag_pairs, axis=1))
        new_state = lax.dot_general(b_g, xend_bf[:, g * gw:(g + 1) * gw], (((0,), (0,)), ((), ())),
                                    preferred_element_type=F32)
        state_ref[g] = state * chunk_decay_x[:, g * gw:(g + 1) * gw] + new_state

    y = jnp.concatenate(y_groups, axis=1) + dskip_ref[...] * xs
    z = z_ref[...]
    gated = y * (z * (1.0 / (1.0 + jnp.exp(-z))))
    gn = d_inner // SSM_GROUPS
    normed = []
    for g in range(SSM_GROUPS):
        gg = gated[:, g * gn:(g + 1) * gn]
        normed.append(gg * lax.rsqrt(jnp.mean(gg * gg, axis=-1, keepdims=True) + NORM_EPS))
    o_ref[...] = (jnp.concatenate(normed, axis=1) * gate_ref[...]).astype(o_ref.dtype)


def _ssd(rest, dt_t, dt_bias, a_log, conv_w, conv_b, d_skip, gate_norm, bsz, seq, n_heads, d_inner,
         z_col, xbc_col, dt_col_block):
    t = rest.shape[0]
    nc = seq // SSD_CHUNK
    width, conv_ch = conv_w.shape
    pad = LANES - n_heads
    row = lambda v: jnp.pad(v, (0, pad)).reshape(1, LANES)
    colv = lambda v: v.reshape(n_heads, 1)
    zb = z_col // d_inner
    tc = math.gcd(xbc_col, conv_ch)
    n_xbc = conv_ch // tc
    const = lambda shape: pl.BlockSpec(shape, lambda b, c: (0,) * len(shape))
    chunk = lambda cols, blk: pl.BlockSpec((SSD_CHUNK, cols), lambda b, c: (b * nc + c, blk))
    return pl.pallas_call(
        functools.partial(_ssd_kernel, n_heads=n_heads, d_inner=d_inner, n_xbc=n_xbc),
        out_shape=jax.ShapeDtypeStruct((t, d_inner), BF16),
        grid=(bsz, nc),
        in_specs=[chunk(tc, xbc_col // tc + j) for j in range(n_xbc)] + [
            chunk(d_inner, zb), chunk(LANES, dt_col_block),
            pl.BlockSpec((None, n_heads, SSD_CHUNK), lambda b, c: (b, 0, c)),
            const((1, LANES)), const((n_heads, 1)), const((1, LANES)), const((n_heads, 1)),
            const((width, conv_ch)), const((1, conv_ch)), const((1, d_inner)), const((1, d_inner))],
        out_specs=pl.BlockSpec((SSD_CHUNK, d_inner), lambda b, c: (b * nc + c, 0)),
        scratch_shapes=[pltpu.VMEM((SSM_GROUPS, SSM_STATE, d_inner // SSM_GROUPS), F32),
                        pltpu.VMEM((SSD_CHUNK, conv_ch), F32), pltpu.VMEM((CONV_TAIL, conv_ch), F32)],
        compiler_params=_cparams(("parallel", "arbitrary")),
        name="ssd_chunk_scan",
    )(*([rest] * n_xbc), rest, rest, dt_t, row(dt_bias), colv(dt_bias), row(a_log), colv(a_log),
      conv_w, conv_b.reshape(1, conv_ch), jnp.repeat(d_skip, HEAD_DIM).reshape(1, d_inner),
      gate_norm.reshape(1, d_inner))


def _even_mixer(h, w_in, w_out, layer, post_w, resid, next_w, bsz, seq):
    d = h.shape[1]
    n_heads = d // (2 * HEAD_DIM)
    hw = n_heads * HEAD_DIM
    qkv_a = _matmul_f32_weight(h, w_in, layer, 0, 3 * hw, BF16)
    qkv_b = _matmul_f32_weight(h, w_in, layer, 3 * hw, 3 * hw, F32, tm=1024)
    oa = _stick_breaking(qkv_a, bsz, seq, n_heads, 0, hw, 2 * hw)
    slopes = jnp.exp2(-ALIBI_MAX_EXP * jnp.arange(1, n_heads + 1, dtype=F32) / n_heads)
    ob = _dilated_attention(qkv_b, slopes, bsz, seq, n_heads)
    return _matmul_norm_residual([oa, ob], w_out, layer, post_w, resid, next_w, tk=512)


def _odd_mixer(h, w_in_all, layer, b_f, conv_w, conv_b, dt_bias, a_log, d_skip, gate_norm, w_out,
               post_w, resid, next_w, bsz, seq):
    d = h.shape[1]
    n_heads = d // (2 * HEAD_DIM)
    hw = n_heads * HEAD_DIM
    d_inner = hw
    conv_ch = d_inner + 2 * SSM_GROUPS * SSM_STATE
    pad = LANES - n_heads
    c0 = 3 * hw
    w_in = w_in_all[layer]
    f_w = w_in[:, c0:c0 + n_heads]
    z_w = w_in[:, c0 + n_heads:c0 + n_heads + d_inner]
    xbc_w = w_in[:, c0 + n_heads + d_inner:c0 + n_heads + d_inner + conv_ch]
    dt_w = w_in[:, c0 + n_heads + d_inner + conv_ch:]
    w_rest = jnp.concatenate([z_w, xbc_w, jnp.pad(f_w, ((0, 0), (0, pad))),
                              jnp.pad(dt_w, ((0, 0), (0, pad)))], axis=1).astype(BF16)
    qkv = _matmul(h, w_in[:, :c0].astype(BF16), BF16)
    rest = _matmul(h, w_rest, F32)
    f_block = (d_inner + conv_ch) // LANES
    dt_block = f_block + 1

    f_cum, f_floor = _logf_cumsum(rest, jnp.pad(b_f, (0, pad)).reshape(1, LANES), bsz, seq, f_block,
                                  FOX_BLOCK)
    oc = _forgetting_attention(qkv, f_cum, f_floor, bsz, seq, n_heads)

    dt_col = dt_block * LANES
    dt_t = rest[:, dt_col:dt_col + n_heads].reshape(bsz, seq, n_heads).transpose(0, 2, 1)
    y = _ssd(rest, dt_t, dt_bias, a_log, conv_w, conv_b, d_skip, gate_norm, bsz, seq, n_heads,
             d_inner, 0, d_inner, dt_block)
    return _matmul_norm_residual([oc, y], w_out, layer, post_w, resid, next_w, tk=512)


def _mlp(h, w_up, w_down, layer, post_w, resid, next_w):
    hidden = _matmul_f32_weight(h, w_up, layer, 0, w_up.shape[2], BF16, relu2=True)
    return _matmul_norm_residual([hidden], w_down, layer, post_w, resid, next_w)


def kernel(x, mix_norm_pre, mix_norm_post, mlp_norm_pre, mlp_norm_post, ab_w_in, ab_w_out,
           cd_w_in, cd_b_f, cd_conv_w, cd_conv_b, cd_dt_bias, cd_a_log, cd_d_skip,
           cd_gate_norm, cd_w_out, mlp_w_up, mlp_w_down):
    bsz, seq, d = x.shape
    depth = mix_norm_pre.shape[0]
    xf = x.reshape(bsz * seq, d)
    h = _rmsnorm(xf, mix_norm_pre[0])
    ab_w_out_bf, cd_w_out_bf, w_down_bf = (w.astype(BF16) for w in (ab_w_out, cd_w_out, mlp_w_down))
    for layer in range(depth):
        i = layer // 2
        if layer % 2 == 0:
            xf, h = _even_mixer(h, ab_w_in, ab_w_out_bf, i, mix_norm_post[layer], xf,
                                mlp_norm_pre[layer], bsz, seq)
        else:
            xf, h = _odd_mixer(h, cd_w_in, i, cd_b_f[i], cd_conv_w[i], cd_conv_b[i], cd_dt_bias[i],
                               cd_a_log[i], cd_d_skip[i], cd_gate_norm[i], cd_w_out_bf,
                               mix_norm_post[layer], xf, mlp_norm_pre[layer], bsz, seq)
        next_w = mix_norm_pre[layer + 1] if layer + 1 < depth else None
        out = _mlp(h, mlp_w_up, w_down_bf, layer, mlp_norm_post[layer], xf, next_w)
        xf, h = out if next_w is not None else (out, None)
    return xf.reshape(bsz, seq, d)
```

```python
import functools
import math

import jax
import jax.numpy as jnp
from jax import lax
from jax.experimental import pallas as pl
from jax.experimental.pallas import tpu as pltpu

F32 = jnp.float32
BF16 = jnp.bfloat16

LANES = 128
HEAD_DIM = 64
PAIR = 2 * HEAD_DIM
NORM_EPS = 1e-6
NEG_INF = -1e30
DILATED_PATTERNS = ((128, 1), (512, 4), (2048, 16))
DILATED_GROUP = 4
ALIBI_MAX_EXP = 8.0
SSM_GROUPS = 4
SSM_STATE = 128
SSD_CHUNK = 128
SB_LOG_UNDERFLOW = -104.0
FOX_LOG_UNDERFLOW = 104.0
FOX_NORM_SLACK = 1.001
VMEM_LIMIT_BYTES = 56 * 1024 * 1024


def _cparams(sem):
    return pltpu.CompilerParams(dimension_semantics=sem, vmem_limit_bytes=VMEM_LIMIT_BYTES)


def _softplus(x):
    return jnp.maximum(x, 0.0) + jnp.log1p(jnp.exp(-jnp.abs(x)))


def _split_bf16(x, parts):
    out = []
    rem = x
    for _ in range(parts):
        hi = rem.astype(BF16)
        out.append(hi)
        rem = rem - hi.astype(F32)
    return out


def _dot_split(x, m, parts):
    acc = None
    for t in _split_bf16(x, parts):
        d = jnp.dot(t, m, preferred_element_type=F32)
        acc = d if acc is None else acc + d
    return acc


def _split_dot(m, x, parts):
    acc = None
    for t in _split_bf16(x, parts):
        d = jnp.dot(m, t, preferred_element_type=F32)
        acc = d if acc is None else acc + d
    return acc


def _rmsnorm_kernel(x_ref, w_ref, o_ref):
    x = x_ref[...]
    y = x * lax.rsqrt(jnp.mean(x * x, axis=-1, keepdims=True) + NORM_EPS)
    o_ref[...] = (y * w_ref[...]).astype(o_ref.dtype)


def _rmsnorm(x2d, w, tm=512):
    t, d = x2d.shape
    return pl.pallas_call(
        _rmsnorm_kernel,
        out_shape=jax.ShapeDtypeStruct((t, d), BF16),
        grid=(t // tm,),
        in_specs=[pl.BlockSpec((tm, d), lambda i: (i, 0)),
                  pl.BlockSpec((1, d), lambda i: (0, 0))],
        out_specs=pl.BlockSpec((tm, d), lambda i: (i, 0)),
        compiler_params=_cparams(("parallel",)),
        name="rmsnorm",
    )(x2d, w.reshape(1, d))


def _mm_kernel(a_ref, b_ref, o_ref, *, relu2):
    acc = jnp.dot(a_ref[...], b_ref[...], preferred_element_type=F32)
    if relu2:
        acc = jnp.square(jnp.maximum(acc, 0.0))
    o_ref[...] = acc.astype(o_ref.dtype)


def _matmul(a, b, out_dtype, *, relu2=False, tm=1024, tn=1792):
    m, kk = a.shape
    n = b.shape[1]
    tm, tn = min(tm, m), min(tn, n)
    while n % tn:
        tn -= LANES
    return pl.pallas_call(
        functools.partial(_mm_kernel, relu2=relu2),
        out_shape=jax.ShapeDtypeStruct((m, n), out_dtype),
        grid=(m // tm, n // tn),
        in_specs=[pl.BlockSpec((tm, kk), lambda i, j: (i, 0)),
                  pl.BlockSpec((kk, tn), lambda i, j: (0, j))],
        out_specs=pl.BlockSpec((tm, tn), lambda i, j: (i, j)),
        compiler_params=_cparams(("parallel", "parallel")),
        name="matmul",
    )(a, b)


def _mm_f32w_kernel(a_ref, b_ref, o_ref, w_ref, *, relu2):
    @pl.when(pl.program_id(1) == 0)
    def _():
        w_ref[...] = b_ref[...].astype(BF16)

    acc = jnp.dot(a_ref[...], w_ref[...], preferred_element_type=F32)
    if relu2:
        acc = jnp.square(jnp.maximum(acc, 0.0))
    o_ref[...] = acc.astype(o_ref.dtype)


def _matmul_f32_weight(a, w, layer, col0, n, out_dtype, *, relu2=False, tm=2048, tn=1024):
    m, kk = a.shape
    tm = min(tm, m)
    g = math.gcd(n, col0)
    tn = max(t for t in range(LANES, min(tn, g) + 1, LANES) if g % t == 0)
    jb = col0 // tn
    return pl.pallas_call(
        functools.partial(_mm_f32w_kernel, relu2=relu2),
        out_shape=jax.ShapeDtypeStruct((m, n), out_dtype),
        grid=(n // tn, m // tm),
        in_specs=[pl.BlockSpec((tm, kk), lambda j, i: (i, 0)),
                  pl.BlockSpec((None, kk, tn), lambda j, i: (layer, 0, jb + j))],
        out_specs=pl.BlockSpec((tm, tn), lambda j, i: (i, j)),
        scratch_shapes=[pltpu.VMEM((kk, tn), BF16)],
        compiler_params=_cparams(("arbitrary", "arbitrary")),
        name="matmul_f32_weight",
    )(a, w)


def _mm_norm_res_kernel(*refs, steps, n_chunk, m_chunk, with_next):
    n_parts = len(steps)
    a_refs = refs[:n_parts]
    b_ref, w_ref, r_ref = refs[n_parts:n_parts + 3]
    rest = refs[n_parts + 3:]
    wn_ref, o_ref, h_ref = rest if with_next else (None, rest[0], None)
    k = pl.program_id(1)
    nk = sum(steps)
    tm, n = o_ref.shape

    @pl.when(k == 0)
    def _():
        o_ref[...] = jnp.zeros_like(o_ref)

    def accumulate(a_ref):
        a = a_ref[...]
        for n0 in range(0, n, n_chunk):
            o_ref[:, n0:n0 + n_chunk] += jnp.dot(a, b_ref[:, n0:n0 + n_chunk],
                                                 preferred_element_type=F32)

    lo = 0
    for a_ref, cnt in zip(a_refs, steps):
        if n_parts == 1:
            accumulate(a_ref)
        else:
            pl.when(jnp.logical_and(k >= lo, k < lo + cnt))(functools.partial(accumulate, a_ref))
        lo += cnt

    @pl.when(k == nk - 1)
    def _():
        for m0 in range(0, tm, m_chunk):
            rows = slice(m0, m0 + m_chunk)
            y = o_ref[rows, :]
            y = y * lax.rsqrt(jnp.mean(y * y, axis=-1, keepdims=True) + NORM_EPS)
            x_new = r_ref[rows, :] + y * w_ref[...]
            o_ref[rows, :] = x_new
            if with_next:
                hn = x_new * lax.rsqrt(jnp.mean(x_new * x_new, axis=-1, keepdims=True) + NORM_EPS)
                h_ref[rows, :] = (hn * wn_ref[...]).astype(h_ref.dtype)


def _matmul_norm_residual(a_parts, b, layer, w, resid, next_w=None, *, tm=1024, tk=1024):
    m = a_parts[0].shape[0]
    n = b.shape[2]
    tm = min(tm, m)
    tk = min([tk] + [a.shape[1] for a in a_parts])
    steps = tuple(a.shape[1] // tk for a in a_parts)
    starts = [sum(steps[:p]) for p in range(len(steps))]
    with_next = next_w is not None

    def a_spec(start, cnt):
        return pl.BlockSpec((tm, tk), lambda i, k: (i, jnp.clip(k - start, 0, cnt - 1)))

    row_spec = pl.BlockSpec((1, n), lambda i, k: (0, 0))
    tile_spec = pl.BlockSpec((tm, n), lambda i, k: (i, 0))
    in_specs = [a_spec(s, c) for s, c in zip(starts, steps)]
    in_specs += [pl.BlockSpec((None, tk, n), lambda i, k: (layer, k, 0)), row_spec, tile_spec]
    args = list(a_parts) + [b, w.reshape(1, n), resid]
    out_shape = jax.ShapeDtypeStruct((m, n), F32)
    out_specs = tile_spec
    if with_next:
        in_specs.append(row_spec)
        args.append(next_w.reshape(1, n))
        out_shape = (out_shape, jax.ShapeDtypeStruct((m, n), BF16))
        out_specs = (tile_spec, tile_spec)
    return pl.pallas_call(
        functools.partial(_mm_norm_res_kernel, steps=steps, n_chunk=min(n, 512),
                          m_chunk=min(tm, 256), with_next=with_next),
        out_shape=out_shape,
        grid=(m // tm, sum(steps)),
        in_specs=in_specs,
        out_specs=out_specs,
        compiler_params=_cparams(("parallel", "arbitrary")),
        name="matmul_norm_residual",
    )(*args)


def _head_masks(rows):
    lane = lax.broadcasted_iota(jnp.int32, (rows, PAIR), 1)
    return lane < HEAD_DIM


SB_BLOCK = 256
SB_LOOKBACK = 256
SB_GROUP = 2


def _sb_kernel(q_ref, k_ref, v_ref, upper_ref, o_ref, acc_ref, c_ref, *, seq):
    tq, back = SB_BLOCK, SB_LOOKBACK
    span = back + tq
    first = _head_masks(tq)
    row = lax.broadcasted_iota(jnp.int32, (tq, span), 0)
    col = lax.broadcasted_iota(jnp.int32, (tq, span), 1)
    scale = jnp.asarray(HEAD_DIM ** -0.5, BF16)

    def split_heads(q):
        return jnp.where(first, q, jnp.zeros_like(q)), jnp.where(first, jnp.zeros_like(q), q)

    def scores(qh, kb, visible=None):
        z = lax.dot_general(qh, kb, (((1,), (1,)), ((), ())), preferred_element_type=F32)
        if visible is not None:
            z = jnp.where(visible, z, NEG_INF)
        soft = jnp.log(1.0 + jnp.exp(-jnp.abs(z)))
        log_beta = jnp.minimum(z, 0.0) - soft
        return log_beta, log_beta - z

    def combine(acc0, acc1):
        return jnp.where(first, acc0, acc1).astype(o_ref.dtype)

    def group(t_block, k_block, layout):
        n_rows = max(k_off for k_off, _ in layout) + span
        k_base = pl.multiple_of(k_block * tq, tq)
        k_all = k_ref[pl.ds(k_base, n_rows), :]
        v_all = v_ref[pl.ds(k_base, n_rows), :]
        live = None
        for u, (k_off, lag) in enumerate(layout):
            rows = pl.ds(pl.multiple_of((t_block + u) * tq, tq), tq)
            q_heads = split_heads(q_ref[rows, :] * scale)
            strict = col < row + lag
            kb, vb = k_all[k_off:k_off + span], v_all[k_off:k_off + span]
            accs = []
            for h in range(2):
                log_beta, log_keep = scores(q_heads[h], kb, strict)
                later = jnp.dot(log_keep.astype(BF16), upper_ref[...], preferred_element_type=F32)
                w = jnp.exp(log_beta + later)
                accs.append(jnp.dot(w.astype(BF16), vb, preferred_element_type=F32))
                c = jnp.sum(log_keep, axis=1, keepdims=True)
                acc_ref[u, h] = accs[h]
                c_ref[u, h] = c
                live = c if live is None else jnp.maximum(live, c)
            o_ref[rows, :] = combine(accs[0], accs[1])

        @pl.when(jnp.max(live) > SB_LOG_UNDERFLOW)
        def _():
            for u, (k_off, _) in enumerate(layout):
                rows = pl.ds(pl.multiple_of((t_block + u) * tq, tq), tq)
                q_heads = split_heads(q_ref[rows, :] * scale)

                def cond(carry):
                    j, alive = carry
                    return jnp.logical_and(j >= 0, alive)

                def walk(carry):
                    j, _ = carry
                    key_rows = pl.ds(pl.multiple_of(j * tq, tq), tq)
                    kb, vb = k_ref[key_rows, :], v_ref[key_rows, :]
                    for h in range(2):
                        log_beta, log_keep = scores(q_heads[h], kb)
                        later = jnp.dot(log_keep.astype(BF16), upper_ref[0:tq, 0:tq],
                                        preferred_element_type=F32)
                        c = c_ref[u, h]
                        w = jnp.exp(log_beta + later + c)
                        acc_ref[u, h] += jnp.dot(w.astype(BF16), vb, preferred_element_type=F32)
                        c_ref[u, h] = c + jnp.sum(log_keep, axis=1, keepdims=True)
                    return j - 1, jnp.max(c_ref[u]) > SB_LOG_UNDERFLOW

                lax.while_loop(cond, walk, (k_block + k_off // tq - 1,
                                            jnp.max(c_ref[u]) > SB_LOG_UNDERFLOW))
                o_ref[rows, :] = combine(acc_ref[u, 0], acc_ref[u, 1])

    back_blocks = back // tq
    head_layout = [(max(u - back_blocks, 0) * tq, min(u, back_blocks) * tq) for u in range(SB_GROUP)]
    group(0, 0, head_layout)

    def body(n, _):
        group(n * SB_GROUP, n * SB_GROUP - back_blocks, [(u * tq, back) for u in range(SB_GROUP)])
        return 0

    lax.fori_loop(1, seq // (SB_GROUP * tq), body, 0)


def _stick_breaking(proj, bsz, seq, n_heads, q_col, k_col, v_col):
    t = proj.shape[0]
    n_pairs = n_heads // 2
    qb, kb, vb = q_col // PAIR, k_col // PAIR, v_col // PAIR
    span = SB_LOOKBACK + SB_BLOCK
    upper = (jnp.arange(span)[:, None] > jnp.arange(span)[None, :]).astype(BF16)
    seq_spec = lambda off: pl.BlockSpec((seq, PAIR), lambda b, p: (b, off + p))
    return pl.pallas_call(
        functools.partial(_sb_kernel, seq=seq),
        out_shape=jax.ShapeDtypeStruct((t, n_heads * HEAD_DIM), BF16),
        grid=(bsz, n_pairs),
        in_specs=[seq_spec(qb), seq_spec(kb), seq_spec(vb),
                  pl.BlockSpec((span, span), lambda b, p: (0, 0))],
        out_specs=pl.BlockSpec((seq, PAIR), lambda b, p: (b, p)),
        scratch_shapes=[pltpu.VMEM((SB_GROUP, 2, SB_BLOCK, PAIR), F32),
                        pltpu.VMEM((SB_GROUP, 2, SB_BLOCK, 1), F32)],
        compiler_params=_cparams(("parallel", "parallel")),
        name="stick_breaking_attention",
    )(proj, proj, proj, upper)


def _dilated_kernel(slope_ref, q_ref, k_ref, v_ref, o_ref, num_ref, den_ref, max_ref, *, seq, tq,
                    n_sub):
    pair = pl.program_id(1)
    first = _head_masks(tq)
    span = 2 * tq
    group = n_sub * tq
    first_keys = _head_masks(group + tq)
    ones = jnp.ones((group + tq, PAIR), BF16)
    row = lax.broadcasted_iota(jnp.int32, (tq, span), 0)
    col = lax.broadcasted_iota(jnp.int32, (tq, span), 1)
    scale = HEAD_DIM ** -0.5

    for branch, (window, r) in enumerate(DILATED_PATTERNS):
        assert window // r == tq
        per_class = seq // (r * group)
        stride = None if r == 1 else r

        def bias(h, class_start):
            hops = tq + row - col
            valid = jnp.logical_and(hops >= 0, hops <= tq)
            if class_start:
                valid = jnp.logical_and(valid, col >= tq)
            return jnp.where(valid, -(slope_ref[2 * pair + h] * (hops * r).astype(F32)), NEG_INF)

        bias_rest = [bias(h, False) for h in range(2)]
        bias_head = [bias(h, True) for h in range(2)]

        def body(n, _):
            cls = n // per_class
            i = n - cls * per_class
            m0 = i * group
            rows = pl.ds(cls + r * m0, group, stride=stride)
            before = pl.ds(cls + r * jnp.maximum(m0 - tq, 0), tq, stride=stride)
            q_all = (q_ref[rows, :] * scale).astype(BF16)
            k_all = jnp.concatenate([k_ref[before, :], k_ref[rows, :]], axis=0).astype(BF16)
            v_all = jnp.concatenate([v_ref[before, :], v_ref[rows, :]], axis=0).astype(BF16)
            v_heads = (jnp.where(first_keys, v_all, ones), jnp.where(first_keys, ones, v_all))
            nums, dens, maxes = [], [], []
            for u in range(n_sub):
                q = q_all[u * tq:(u + 1) * tq]
                kc = k_all[u * tq:u * tq + span]
                q_heads = (jnp.where(first, q, jnp.zeros_like(q)),
                           jnp.where(first, jnp.zeros_like(q), q))
                res, mxs = [], []
                for h in range(2):
                    z = lax.dot_general(q_heads[h], kc, (((1,), (1,)), ((), ())),
                                        preferred_element_type=F32)
                    b = bias_rest[h] if u else jnp.where(i == 0, bias_head[h], bias_rest[h])
                    logits = z + b
                    m = jnp.max(logits, axis=-1, keepdims=True)
                    p = jnp.exp(logits - m)
                    res.append(jnp.dot(p.astype(BF16), v_heads[h][u * tq:u * tq + span],
                                       preferred_element_type=F32))
                    mxs.append(jnp.broadcast_to(m, (tq, PAIR)))
                nums.append(jnp.where(first, res[0], res[1]))
                dens.append(pltpu.roll(jnp.where(first, res[1], res[0]), HEAD_DIM, 1))
                maxes.append(jnp.where(first, mxs[0], mxs[1]))
            num = jnp.concatenate(nums, axis=0)
            den = jnp.concatenate(dens, axis=0)
            mx = jnp.concatenate(maxes, axis=0)
            if branch == 0:
                num_ref[rows, :] = num
                den_ref[rows, :] = den
                max_ref[rows, :] = mx
            else:
                m_old = max_ref[rows, :]
                m_new = jnp.maximum(m_old, mx)
                w_old = jnp.exp(m_old - m_new)
                w_new = jnp.exp(mx - m_new)
                num_ref[rows, :] = w_old * num_ref[rows, :] + w_new * num
                den_ref[rows, :] = w_old * den_ref[rows, :] + w_new * den
                max_ref[rows, :] = m_new
            return 0

        lax.fori_loop(0, seq // group, body, 0, unroll=2)

    o_ref[...] = (num_ref[...] / den_ref[...]).astype(o_ref.dtype)


def _dilated_attention(qkv, slopes, bsz, seq, n_heads):
    t = qkv.shape[0]
    n_pairs = n_heads // 2
    tq = DILATED_PATTERNS[0][0] // DILATED_PATTERNS[0][1]
    shortest_class = seq // max(r for _, r in DILATED_PATTERNS)
    n_sub = min(DILATED_GROUP, shortest_class // tq)
    spec = lambda off: pl.BlockSpec((seq, PAIR), lambda b, p: (b, off + p))
    return pl.pallas_call(
        functools.partial(_dilated_kernel, seq=seq, tq=tq, n_sub=n_sub),
        out_shape=jax.ShapeDtypeStruct((t, n_heads * HEAD_DIM), BF16),
        grid=(bsz, n_pairs),
        in_specs=[pl.BlockSpec(memory_space=pltpu.SMEM), spec(0), spec(n_pairs), spec(2 * n_pairs)],
        out_specs=pl.BlockSpec((seq, PAIR), lambda b, p: (b, p)),
        scratch_shapes=[pltpu.VMEM((seq, PAIR), F32)] * 3,
        compiler_params=_cparams(("parallel", "parallel")),
        name="dilated_window_attention",
    )(slopes, qkv, qkv, qkv)


def _logf_cumsum_kernel(f_ref, b_ref, o_ref, floor_ref, carry_ref, low_ref, *, tb):
    i = pl.program_id(1)

    @pl.when(i == 0)
    def _():
        carry_ref[...] = jnp.zeros_like(carry_ref)
        low_ref[...] = jnp.zeros_like(low_ref)

    x = f_ref[...] + b_ref[...]
    log_f = jnp.minimum(x, 0.0) - jnp.log1p(jnp.exp(-jnp.abs(x)))
    row = lax.broadcasted_iota(jnp.int32, (tb, tb), 0)
    col = lax.broadcasted_iota(jnp.int32, (tb, tb), 1)
    lower = jnp.where(col <= row, 1.0, 0.0).astype(BF16)
    cum = _split_dot(lower, log_f, 3) + carry_ref[...]
    o_ref[...] = cum
    carry_ref[...] = cum[tb - 1:tb, :]
    low = jnp.minimum(low_ref[...], jnp.min(cum, axis=0, keepdims=True))
    low_ref[...] = low
    floor_ref[pl.ds(i, 1), :] = low


def _logf_cumsum(rest, b_f_row, bsz, seq, col_block, tb):
    t = rest.shape[0]
    nb = seq // tb
    return pl.pallas_call(
        functools.partial(_logf_cumsum_kernel, tb=tb),
        out_shape=(jax.ShapeDtypeStruct((t, LANES), F32),
                   jax.ShapeDtypeStruct((bsz, nb, LANES), F32)),
        grid=(bsz, nb),
        in_specs=[pl.BlockSpec((tb, LANES), lambda b, i: (b * nb + i, col_block)),
                  pl.BlockSpec((1, LANES), lambda b, i: (0, 0))],
        out_specs=(pl.BlockSpec((tb, LANES), lambda b, i: (b * nb + i, 0)),
                   pl.BlockSpec((None, nb, LANES), lambda b, i: (b, 0, 0))),
        scratch_shapes=[pltpu.VMEM((1, LANES), F32), pltpu.VMEM((1, LANES), F32)],
        compiler_params=_cparams(("parallel", "arbitrary")),
        name="log_forget_cumsum",
    )(rest, b_f_row)


FOX_BLOCK = 1024
FOX_F_PARTS = 3


def _fox_kernel(q_ref, k_ref, v_ref, f_ref, floor_ref, o_ref, acc_ref, m_ref, ka_ref, knorm_ref,
                *, tq, seq):
    pair = pl.program_id(1)
    qi = pl.program_id(2)
    reps = tq // LANES
    first = _head_masks(tq)
    lane = lax.broadcasted_iota(jnp.int32, (tq, PAIR), 1)
    f_base = (HEAD_DIM, 0)
    f_lanes = [jnp.logical_and(lane >= f_base[h], lane < f_base[h] + FOX_F_PARTS) for h in range(2)]

    @pl.when(qi == 0)
    def _():
        src = lax.broadcasted_iota(jnp.int32, (LANES, PAIR), 0)
        dst = lax.broadcasted_iota(jnp.int32, (LANES, PAIR), 1)
        place = [[jnp.where(jnp.logical_and(src == 2 * pair + h, dst == f_base[h] + t), 1.0, 0.0)
                  .astype(BF16) for t in range(FOX_F_PARTS)] for h in range(2)]

        def prepare(c, carry):
            rows = pl.ds(pl.multiple_of(c * tq, tq), tq)
            kb = k_ref[rows, :]
            parts = _split_bf16(f_ref[rows, :], FOX_F_PARTS)
            for h in range(2):
                f_terms = None
                for t in range(FOX_F_PARTS):
                    d = jnp.dot(parts[t], place[h][t], preferred_element_type=F32)
                    f_terms = d if f_terms is None else f_terms + d
                own = first if h == 0 else jnp.logical_not(first)
                ka_ref[h, rows, :] = jnp.where(own, kb, (-f_terms).astype(BF16))
            sq = kb.astype(F32)
            sq = sq * sq
            n0 = jnp.sum(jnp.where(first, sq, 0.0), axis=1, keepdims=True)
            n1 = jnp.sum(jnp.where(first, 0.0, sq), axis=1, keepdims=True)
            return jnp.maximum(carry[0], n0), jnp.maximum(carry[1], n1)

        zero = jnp.zeros((tq, 1), F32)
        n0, n1 = lax.fori_loop(0, seq // tq, prepare, (zero, zero))
        knorm_ref[0] = jnp.sqrt(jnp.max(n0))
        knorm_ref[1] = jnp.sqrt(jnp.max(n1))

    q = q_ref[...] * jnp.asarray(HEAD_DIM ** -0.5, BF16)
    zeros = jnp.zeros_like(q)
    q_heads = (jnp.where(first, q, zeros), jnp.where(first, zeros, q))
    one = jnp.ones_like(q)
    q_aug = [jnp.where(f_lanes[h], one, q_heads[h]) for h in range(2)]
    row = lax.broadcasted_iota(jnp.int32, (tq, tq), 0)
    col = lax.broadcasted_iota(jnp.int32, (tq, tq), 1)
    causal = col <= row
    ones = jnp.ones((tq, PAIR), BF16)

    acc_ref[...] = jnp.zeros_like(acc_ref)
    m_ref[...] = jnp.full_like(m_ref, NEG_INF)

    def step(j, masked):
        rows = pl.ds(pl.multiple_of(j * tq, tq), tq)
        vb = v_ref[rows, :]
        v_heads = (jnp.where(first, vb, ones), jnp.where(first, ones, vb))
        for h in range(2):
            s = lax.dot_general(q_aug[h], ka_ref[h, rows, :], (((1,), (1,)), ((), ())),
                                preferred_element_type=F32)
            if masked:
                s = jnp.where(causal, s, NEG_INF)
            m_old = m_ref[h]
            m_new = jnp.maximum(m_old, jnp.max(s, axis=-1, keepdims=True))
            alpha = jnp.exp(m_old - m_new)
            p = jnp.exp(s - jnp.tile(m_new, (1, reps)))
            acc_ref[h] = alpha * acc_ref[h] + jnp.dot(p.astype(BF16), v_heads[h],
                                                      preferred_element_type=F32)
            m_ref[h] = m_new

    step(qi, True)

    floors = floor_ref[...]
    blk = lax.broadcasted_iota(jnp.int32, floors.shape, 0)
    head_lane = lax.broadcasted_iota(jnp.int32, floors.shape, 1)
    skip = []
    for h in range(2):
        q32 = q_heads[h].astype(F32)
        q_norm = jnp.sqrt(jnp.sum(q32 * q32, axis=1, keepdims=True))
        reach = q_norm * (knorm_ref[h] * FOX_NORM_SLACK) - m_ref[h]
        limit = jnp.max(reach) + FOX_LOG_UNDERFLOW
        dead = jnp.logical_and(jnp.logical_and(head_lane == 2 * pair + h, blk < qi), floors > limit)
        skip.append(jnp.sum(jnp.where(dead, 1, 0)))
    start = jnp.minimum(skip[0], skip[1])

    def body(j, _):
        step(j, False)
        return 0

    lax.fori_loop(start, qi, body, 0)
    outs = [acc_ref[h] / pltpu.roll(acc_ref[h], HEAD_DIM, 1) for h in range(2)]
    o_ref[...] = jnp.where(first, outs[0], outs[1]).astype(o_ref.dtype)


def _forgetting_attention(qkv, f_cum, f_floor, bsz, seq, n_heads):
    t = qkv.shape[0]
    tq = FOX_BLOCK
    n_pairs = n_heads // 2
    nq = seq // tq
    return pl.pallas_call(
        functools.partial(_fox_kernel, tq=tq, seq=seq),
        out_shape=jax.ShapeDtypeStruct((t, n_heads * HEAD_DIM), BF16),
        grid=(bsz, n_pairs, nq),
        in_specs=[pl.BlockSpec((tq, PAIR), lambda b, p, i: (b * nq + i, p)),
                  pl.BlockSpec((seq, PAIR), lambda b, p, i: (b, n_pairs + p)),
                  pl.BlockSpec((seq, PAIR), lambda b, p, i: (b, 2 * n_pairs + p)),
                  pl.BlockSpec((seq, LANES), lambda b, p, i: (b, 0)),
                  pl.BlockSpec((None, nq, LANES), lambda b, p, i: (b, 0, 0))],
        out_specs=pl.BlockSpec((tq, PAIR), lambda b, p, i: (b * nq + i, p)),
        scratch_shapes=[pltpu.VMEM((2, tq, PAIR), F32), pltpu.VMEM((2, tq, LANES), F32),
                        pltpu.VMEM((2, seq, PAIR), BF16), pltpu.SMEM((2,), F32)],
        compiler_params=_cparams(("arbitrary", "arbitrary", "arbitrary")),
        name="forgetting_attention",
    )(qkv, qkv, qkv, f_cum, f_floor)


CONV_TAIL = 8


def _ssd_kernel(*refs, n_heads, d_inner, n_xbc):
    xbc_refs = refs[:n_xbc]
    (z_ref, dt_ref, dtt_ref, bias_row_ref, bias_col_ref, alog_row_ref, alog_col_ref, convw_ref,
     convb_ref, dskip_ref, gate_ref, o_ref, state_ref, xc_ref, tail_ref) = refs[n_xbc:]
    q = SSD_CHUNK
    n = SSM_STATE
    hpg = n_heads // SSM_GROUPS
    gw = hpg * HEAD_DIM

    @pl.when(pl.program_id(1) == 0)
    def _():
        state_ref[...] = jnp.zeros_like(state_ref)
        tail_ref[...] = jnp.zeros_like(tail_ref)

    width = convw_ref.shape[0]
    x_raw = jnp.concatenate([r[...] for r in xbc_refs], axis=1)
    xx = jnp.concatenate([tail_ref[...], x_raw], axis=0)
    taps = convw_ref[...]
    y = convb_ref[...] + taps[width - 1:width, :] * x_raw
    for tap in range(width - 1):
        start = CONV_TAIL - (width - 1 - tap)
        y = y + taps[tap:tap + 1, :] * xx[start:start + q, :]
    xc_ref[...] = y * (1.0 / (1.0 + jnp.exp(-y)))
    tail_ref[...] = x_raw[q - CONV_TAIL:q, :]

    row = lax.broadcasted_iota(jnp.int32, (q, q), 0)
    col = lax.broadcasted_iota(jnp.int32, (q, q), 1)
    causal = col <= row
    lower = jnp.where(causal, 1.0, 0.0).astype(BF16)
    upper = jnp.where(row <= col, 1.0, 0.0).astype(BF16)
    er = lax.broadcasted_iota(jnp.int32, (LANES, d_inner), 0)
    ec = lax.broadcasted_iota(jnp.int32, (LANES, d_inner), 1)
    expand = jnp.where(ec // HEAD_DIM == er, 1.0, 0.0).astype(BF16)
    first = _head_masks(q)

    dt = _softplus(dt_ref[...] + bias_row_ref[...])
    a_cum = _split_dot(lower, dt * (-jnp.exp(alog_row_ref[...])), 3)
    a_last = a_cum[q - 1:q, :]
    dt_t = _softplus(dtt_ref[...] + bias_col_ref[...])
    a_cum_t = _dot_split(dt_t * (-jnp.exp(alog_col_ref[...])), upper, 3)

    dt_x = _dot_split(dt, expand, 2)
    decay_in_x = _dot_split(jnp.exp(a_cum), expand, 2)
    decay_out_x = _dot_split(jnp.exp(a_last - a_cum), expand, 2)
    chunk_decay_x = _dot_split(jnp.broadcast_to(jnp.exp(a_last), (8, LANES)), expand, 2)[0:1, :]

    xs = xc_ref[:, 0:d_inner]
    xdt = xs * dt_x
    xdt_bf = xdt.astype(BF16)
    xend_bf = (xdt * decay_out_x).astype(BF16)

    y_groups = []
    for g in range(SSM_GROUPS):
        b_g = xc_ref[:, d_inner + g * n:d_inner + (g + 1) * n].astype(BF16)
        c_g = xc_ref[:, d_inner + (SSM_GROUPS + g) * n:d_inner + (SSM_GROUPS + g + 1) * n].astype(BF16)
        cb = lax.dot_general(c_g, b_g, (((1,), (1,)), ((), ())), preferred_element_type=F32)
        state = state_ref[g]
        y_off = jnp.dot(c_g, state.astype(BF16), preferred_element_type=F32)
        y_g = y_off * decay_in_x[:, g * gw:(g + 1) * gw]
        diag_pairs = []
        for pr in range(hpg // 2):
            lo = g * gw + pr * PAIR
            x_pair = xdt_bf[:, lo:lo + PAIR]
            acc = None
            for hh in range(2):
                h = g * hpg + pr * 2 + hh
                seg = a_cum[:, h:h + 1] - a_cum_t[h:h + 1, :]
                m = (cb * jnp.where(causal, jnp.exp(seg), 0.0)).astype(BF16)
                x_h = jnp.where(first, x_pair, jnp.zeros_like(x_pair)) if hh == 0 else \
                    jnp.where(first, jnp.zeros_like(x_pair), x_pair)
                d = jnp.dot(m, x_h, preferred_element_type=F32)
                acc = d if acc is None else acc + d
            diag_pairs.append(acc)
        y_groups.append(y_g + jnp.concatenate(diag_pairs, axis=1))
        new_state = lax.dot_general(b_g, xend_bf[:, g * gw:(g + 1) * gw], (((0,), (0,)), ((), ())),
                                    preferred_element_type=F32)
        state_ref[g] = state * chunk_decay_x[:, g * gw:(g + 1) * gw] + new_state

    y = jnp.concatenate(y_groups, axis=1) + dskip_ref[...] * xs
    z = z_ref[...]
    gated = y * (z * (1.0 / (1.0 + jnp.exp(-z))))
    gn = d_inner // SSM_GROUPS
    normed = []
    for g in range(SSM_GROUPS):
        gg = gated[:, g * gn:(g + 1) * gn]
        normed.append(gg * lax.rsqrt(jnp.mean(gg * gg, axis=-1, keepdims=True) + NORM_EPS))
    o_ref[...] = (jnp.concatenate(normed, axis=1) * gate_ref[...]).astype(o_ref.dtype)


def _ssd(rest, dt_t, dt_bias, a_log, conv_w, conv_b, d_skip, gate_norm, bsz, seq, n_heads, d_inner,
         z_col, xbc_col, dt_col_block):
    t = rest.shape[0]
    nc = seq // SSD_CHUNK
    width, conv_ch = conv_w.shape
    pad = LANES - n_heads
    row = lambda v: jnp.pad(v, (0, pad)).reshape(1, LANES)
    colv = lambda v: v.reshape(n_heads, 1)
    zb = z_col // d_inner
    tc = math.gcd(xbc_col, conv_ch)
    n_xbc = conv_ch // tc
    const = lambda shape: pl.BlockSpec(shape, lambda b, c: (0,) * len(shape))
    chunk = lambda cols, blk: pl.BlockSpec((SSD_CHUNK, cols), lambda b, c: (b * nc + c, blk))
    return pl.pallas_call(
        functools.partial(_ssd_kernel, n_heads=n_heads, d_inner=d_inner, n_xbc=n_xbc),
        out_shape=jax.ShapeDtypeStruct((t, d_inner), BF16),
        grid=(bsz, nc),
        in_specs=[chunk(tc, xbc_col // tc + j) for j in range(n_xbc)] + [
            chunk(d_inner, zb), chunk(LANES, dt_col_block),
            pl.BlockSpec((None, n_heads, SSD_CHUNK), lambda b, c: (b, 0, c)),
            const((1, LANES)), const((n_heads, 1)), const((1, LANES)), const((n_heads, 1)),
            const((width, conv_ch)), const((1, conv_ch)), const((1, d_inner)), const((1, d_inner))],
        out_specs=pl.BlockSpec((SSD_CHUNK, d_inner), lambda b, c: (b * nc + c, 0)),
        scratch_shapes=[pltpu.VMEM((SSM_GROUPS, SSM_STATE, d_inner // SSM_GROUPS), F32),
                        pltpu.VMEM((SSD_CHUNK, conv_ch), F32), pltpu.VMEM((CONV_TAIL, conv_ch), F32)],
        compiler_params=_cparams(("parallel", "arbitrary")),
        name="ssd_chunk_scan",
    )(*([rest] * n_xbc), rest, rest, dt_t, row(dt_bias), colv(dt_bias), row(a_log), colv(a_log),
      conv_w, conv_b.reshape(1, conv_ch), jnp.repeat(d_skip, HEAD_DIM).reshape(1, d_inner),
      gate_norm.reshape(1, d_inner))


def _even_mixer(h, w_in, w_out, layer, post_w, resid, next_w, bsz, seq):
    d = h.shape[1]
    n_heads = d // (2 * HEAD_DIM)
    hw = n_heads * HEAD_DIM
    qkv_a = _matmul_f32_weight(h, w_in, layer, 0, 3 * hw, BF16)
    qkv_b = _matmul_f32_weight(h, w_in, layer, 3 * hw, 3 * hw, F32, tm=1024)
    oa = _stick_breaking(qkv_a, bsz, seq, n_heads, 0, hw, 2 * hw)
    slopes = jnp.exp2(-ALIBI_MAX_EXP * jnp.arange(1, n_heads + 1, dtype=F32) / n_heads)
    ob = _dilated_attention(qkv_b, slopes, bsz, seq, n_heads)
    return _matmul_norm_residual([oa, ob], w_out, layer, post_w, resid, next_w, tk=512)


def _odd_mixer(h, w_in_all, layer, b_f, conv_w, conv_b, dt_bias, a_log, d_skip, gate_norm, w_out,
               post_w, resid, next_w, bsz, seq):
    d = h.shape[1]
    n_heads = d // (2 * HEAD_DIM)
    hw = n_heads * HEAD_DIM
    d_inner = hw
    conv_ch = d_inner + 2 * SSM_GROUPS * SSM_STATE
    pad = LANES - n_heads
    c0 = 3 * hw
    w_in = w_in_all[layer]
    f_w = w_in[:, c0:c0 + n_heads]
    z_w = w_in[:, c0 + n_heads:c0 + n_heads + d_inner]
    xbc_w = w_in[:, c0 + n_heads + d_inner:c0 + n_heads + d_inner + conv_ch]
    dt_w = w_in[:, c0 + n_heads + d_inner + conv_ch:]
    w_rest = jnp.concatenate([z_w, xbc_w, jnp.pad(f_w, ((0, 0), (0, pad))),
                              jnp.pad(dt_w, ((0, 0), (0, pad)))], axis=1).astype(BF16)
    qkv = _matmul(h, w_in[:, :c0].astype(BF16), BF16)
    rest = _matmul(h, w_rest, F32)
    f_block = (d_inner + conv_ch) // LANES
    dt_block = f_block + 1

    f_cum, f_floor = _logf_cumsum(rest, jnp.pad(b_f, (0, pad)).reshape(1, LANES), bsz, seq, f_block,
                                  FOX_BLOCK)
    oc = _forgetting_attention(qkv, f_cum, f_floor, bsz, seq, n_heads)

    dt_col = dt_block * LANES
    dt_t = rest[:, dt_col:dt_col + n_heads].reshape(bsz, seq, n_heads).transpose(0, 2, 1)
    y = _ssd(rest, dt_t, dt_bias, a_log, conv_w, conv_b, d_skip, gate_norm, bsz, seq, n_heads,
             d_inner, 0, d_inner, dt_block)
    return _matmul_norm_residual([oc, y], w_out, layer, post_w, resid, next_w, tk=512)


def _mlp(h, w_up, w_down, layer, post_w, resid, next_w):
    hidden = _matmul_f32_weight(h, w_up, layer, 0, w_up.shape[2], BF16, relu2=True)
    return _matmul_norm_residual([hidden], w_down, layer, post_w, resid, next_w)


def kernel(x, mix_norm_pre, mix_norm_post, mlp_norm_pre, mlp_norm_post, ab_w_in, ab_w_out,
           cd_w_in, cd_b_f, cd_conv_w, cd_conv_b, cd_dt_bias, cd_a_log, cd_d_skip,
           cd_gate_norm, cd_w_out, mlp_w_up, mlp_w_down):
    bsz, seq, d = x.shape
    depth = mix_norm_pre.shape[0]
    xf = x.reshape(bsz * seq, d)
    h = _rmsnorm(xf, mix_norm_pre[0])
    ab_w_out_bf, cd_w_out_bf, w_down_bf = (w.astype(BF16) for w in (ab_w_out, cd_w_out, mlp_w_down))
    for layer in range(depth):
        i = layer // 2
        if layer % 2 == 0:
            xf, h = _even_mixer(h, ab_w_in, ab_w_out_bf, i, mix_norm_post[layer], xf,
                                mlp_norm_pre[layer], bsz, seq)
        else:
            xf, h = _odd_mixer(h, cd_w_in, i, cd_b_f[i], cd_conv_w[i], cd_conv_b[i], cd_dt_bias[i],
                               cd_a_log[i], cd_d_skip[i], cd_gate_norm[i], cd_w_out_bf,
                               mix_norm_post[layer], xf, mlp_norm_pre[layer], bsz, seq)
        next_w = mix_norm_pre[layer + 1] if layer + 1 < depth else None
        out = _mlp(h, mlp_w_up, w_down_bf, layer, mlp_norm_post[layer], xf, next_w)
        xf, h = out if next_w is not None else (out, None)
    return xf.reshape(bsz, seq, d)
```

```python
import functools
import math

import jax
import jax.numpy as jnp
from jax import lax
from jax.experimental import pallas as pl
from jax.experimental.pallas import tpu as pltpu

F32 = jnp.float32
BF16 = jnp.bfloat16

LANES = 128
HEAD_DIM = 64
PAIR = 2 * HEAD_DIM
NORM_EPS = 1e-6
NEG_INF = -1e30
DILATED_PATTERNS = ((128, 1), (512, 4), (2048, 16))
DILATED_GROUP = 4
ALIBI_MAX_EXP = 8.0
SSM_GROUPS = 4
SSM_STATE = 128
SSD_CHUNK = 128
SB_LOG_UNDERFLOW = -104.0
FOX_LOG_UNDERFLOW = 104.0
FOX_NORM_SLACK = 1.001
VMEM_LIMIT_BYTES = 56 * 1024 * 1024


def _cparams(sem):
    return pltpu.CompilerParams(dimension_semantics=sem, vmem_limit_bytes=VMEM_LIMIT_BYTES)


def _softplus(x):
    return jnp.maximum(x, 0.0) + jnp.log1p(jnp.exp(-jnp.abs(x)))


def _split_bf16(x, parts):
    out = []
    rem = x
    for _ in range(parts):
        hi = rem.astype(BF16)
        out.append(hi)
        rem = rem - hi.astype(F32)
    return out


def _dot_split(x, m, parts):
    acc = None
    for t in _split_bf16(x, parts):
        d = jnp.dot(t, m, preferred_element_type=F32)
        acc = d if acc is None else acc + d
    return acc


def _split_dot(m, x, parts):
    acc = None
    for t in _split_bf16(x, parts):
        d = jnp.dot(m, t, preferred_element_type=F32)
        acc = d if acc is None else acc + d
    return acc


def _rmsnorm_kernel(x_ref, w_ref, o_ref):
    x = x_ref[...]
    y = x * lax.rsqrt(jnp.mean(x * x, axis=-1, keepdims=True) + NORM_EPS)
    o_ref[...] = (y * w_ref[...]).astype(o_ref.dtype)


def _rmsnorm(x2d, w, tm=512):
    t, d = x2d.shape
    return pl.pallas_call(
        _rmsnorm_kernel,
        out_shape=jax.ShapeDtypeStruct((t, d), BF16),
        grid=(t // tm,),
        in_specs=[pl.BlockSpec((tm, d), lambda i: (i, 0)),
                  pl.BlockSpec((1, d), lambda i: (0, 0))],
        out_specs=pl.BlockSpec((tm, d), lambda i: (i, 0)),
        compiler_params=_cparams(("parallel",)),
        name="rmsnorm",
    )(x2d, w.reshape(1, d))


def _mm_kernel(a_ref, b_ref, o_ref, *, relu2):
    acc = jnp.dot(a_ref[...], b_ref[...], preferred_element_type=F32)
    if relu2:
        acc = jnp.square(jnp.maximum(acc, 0.0))
    o_ref[...] = acc.astype(o_ref.dtype)


def _matmul(a, b, out_dtype, *, relu2=False, tm=1024, tn=1792):
    m, kk = a.shape
    n = b.shape[1]
    tm, tn = min(tm, m), min(tn, n)
    while n % tn:
        tn -= LANES
    return pl.pallas_call(
        functools.partial(_mm_kernel, relu2=relu2),
        out_shape=jax.ShapeDtypeStruct((m, n), out_dtype),
        grid=(m // tm, n // tn),
        in_specs=[pl.BlockSpec((tm, kk), lambda i, j: (i, 0)),
                  pl.BlockSpec((kk, tn), lambda i, j: (0, j))],
        out_specs=pl.BlockSpec((tm, tn), lambda i, j: (i, j)),
        compiler_params=_cparams(("parallel", "parallel")),
        name="matmul",
    )(a, b)


def _mm_f32w_kernel(a_ref, b_ref, o_ref, w_ref, *, relu2):
    @pl.when(pl.program_id(1) == 0)
    def _():
        w_ref[...] = b_ref[...].astype(BF16)

    acc = jnp.dot(a_ref[...], w_ref[...], preferred_element_type=F32)
    if relu2:
        acc = jnp.square(jnp.maximum(acc, 0.0))
    o_ref[...] = acc.astype(o_ref.dtype)


def _matmul_f32_weight(a, w, layer, col0, n, out_dtype, *, relu2=False, tm=2048, tn=1024):
    m, kk = a.shape
    tm = min(tm, m)
    g = math.gcd(n, col0)
    tn = max(t for t in range(LANES, min(tn, g) + 1, LANES) if g % t == 0)
    jb = col0 // tn
    return pl.pallas_call(
        functools.partial(_mm_f32w_kernel, relu2=relu2),
        out_shape=jax.ShapeDtypeStruct((m, n), out_dtype),
        grid=(n // tn, m // tm),
        in_specs=[pl.BlockSpec((tm, kk), lambda j, i: (i, 0)),
                  pl.BlockSpec((None, kk, tn), lambda j, i: (layer, 0, jb + j))],
        out_specs=pl.BlockSpec((tm, tn), lambda j, i: (i, j)),
        scratch_shapes=[pltpu.VMEM((kk, tn), BF16)],
        compiler_params=_cparams(("arbitrary", "arbitrary")),
        name="matmul_f32_weight",
    )(a, w)


def _mm_norm_res_kernel(*refs, steps, n_chunk, m_chunk, with_next):
    n_parts = len(steps)
    a_refs = refs[:n_parts]
    b_ref, w_ref, r_ref = refs[n_parts:n_parts + 3]
    rest = refs[n_parts + 3:]
    wn_ref, o_ref, h_ref = rest if with_next else (None, rest[0], None)
    k = pl.program_id(1)
    nk = sum(steps)
    tm, n = o_ref.shape

    @pl.when(k == 0)
    def _():
        o_ref[...] = jnp.zeros_like(o_ref)

    def accumulate(a_ref):
        a = a_ref[...]
        for n0 in range(0, n, n_chunk):
            o_ref[:, n0:n0 + n_chunk] += jnp.dot(a, b_ref[:, n0:n0 + n_chunk],
                                                 preferred_element_type=F32)

    lo = 0
    for a_ref, cnt in zip(a_refs, steps):
        if n_parts == 1:
            accumulate(a_ref)
        else:
            pl.when(jnp.logical_and(k >= lo, k < lo + cnt))(functools.partial(accumulate, a_ref))
        lo += cnt

    @pl.when(k == nk - 1)
    def _():
        for m0 in range(0, tm, m_chunk):
            rows = slice(m0, m0 + m_chunk)
            y = o_ref[rows, :]
            y = y * lax.rsqrt(jnp.mean(y * y, axis=-1, keepdims=True) + NORM_EPS)
            x_new = r_ref[rows, :] + y * w_ref[...]
            o_ref[rows, :] = x_new
            if with_next:
                hn = x_new * lax.rsqrt(jnp.mean(x_new * x_new, axis=-1, keepdims=True) + NORM_EPS)
                h_ref[rows, :] = (hn * wn_ref[...]).astype(h_ref.dtype)


def _matmul_norm_residual(a_parts, b, layer, w, resid, next_w=None, *, tm=1024, tk=1024):
    m = a_parts[0].shape[0]
    n = b.shape[2]
    tm = min(tm, m)
    tk = min([tk] + [a.shape[1] for a in a_parts])
    steps = tuple(a.shape[1] // tk for a in a_parts)
    starts = [sum(steps[:p]) for p in range(len(steps))]
    with_next = next_w is not None

    def a_spec(start, cnt):
        return pl.BlockSpec((tm, tk), lambda i, k: (i, jnp.clip(k - start, 0, cnt - 1)))

    row_spec = pl.BlockSpec((1, n), lambda i, k: (0, 0))
    tile_spec = pl.BlockSpec((tm, n), lambda i, k: (i, 0))
    in_specs = [a_spec(s, c) for s, c in zip(starts, steps)]
    in_specs += [pl.BlockSpec((None, tk, n), lambda i, k: (layer, k, 0)), row_spec, tile_spec]
    args = list(a_parts) + [b, w.reshape(1, n), resid]
    out_shape = jax.ShapeDtypeStruct((m, n), F32)
    out_specs = tile_spec
    if with_next:
        in_specs.append(row_spec)
        args.append(next_w.reshape(1, n))
        out_shape = (out_shape, jax.ShapeDtypeStruct((m, n), BF16))
        out_specs = (tile_spec, tile_spec)
    return pl.pallas_call(
        functools.partial(_mm_norm_res_kernel, steps=steps, n_chunk=min(n, 512),
                          m_chunk=min(tm, 256), with_next=with_next),
        out_shape=out_shape,
        grid=(m // tm, sum(steps)),
        in_specs=in_specs,
        out_specs=out_specs,
        compiler_params=_cparams(("parallel", "arbitrary")),
        name="matmul_norm_residual",
    )(*args)


def _head_masks(rows):
    lane = lax.broadcasted_iota(jnp.int32, (rows, PAIR), 1)
    return lane < HEAD_DIM


SB_BLOCK = 256
SB_LOOKBACK = 256
SB_GROUP = 2


def _sb_kernel(q_ref, k_ref, v_ref, upper_ref, o_ref, acc_ref, c_ref, *, seq):
    tq, back = SB_BLOCK, SB_LOOKBACK
    span = back + tq
    first = _head_masks(tq)
    row = lax.broadcasted_iota(jnp.int32, (tq, span), 0)
    col = lax.broadcasted_iota(jnp.int32, (tq, span), 1)
    scale = jnp.asarray(HEAD_DIM ** -0.5, BF16)

    def split_heads(q):
        return jnp.where(first, q, jnp.zeros_like(q)), jnp.where(first, jnp.zeros_like(q), q)

    def scores(qh, kb, visible=None):
        z = lax.dot_general(qh, kb, (((1,), (1,)), ((), ())), preferred_element_type=F32)
        if visible is not None:
            z = jnp.where(visible, z, NEG_INF)
        soft = jnp.log(1.0 + jnp.exp(-jnp.abs(z)))
        log_beta = jnp.minimum(z, 0.0) - soft
        return log_beta, log_beta - z

    def combine(acc0, acc1):
        return jnp.where(first, acc0, acc1).astype(o_ref.dtype)

    def group(t_block, k_block, layout):
        n_rows = max(k_off for k_off, _ in layout) + span
        k_base = pl.multiple_of(k_block * tq, tq)
        k_all = k_ref[pl.ds(k_base, n_rows), :]
        v_all = v_ref[pl.ds(k_base, n_rows), :]
        live = None
        for u, (k_off, lag) in enumerate(layout):
            rows = pl.ds(pl.multiple_of((t_block + u) * tq, tq), tq)
            q_heads = split_heads(q_ref[rows, :] * scale)
            strict = col < row + lag
            kb, vb = k_all[k_off:k_off + span], v_all[k_off:k_off + span]
            accs = []
            for h in range(2):
                log_beta, log_keep = scores(q_heads[h], kb, strict)
                later = jnp.dot(log_keep.astype(BF16), upper_ref[...], preferred_element_type=F32)
                w = jnp.exp(log_beta + later)
                accs.append(jnp.dot(w.astype(BF16), vb, preferred_element_type=F32))
                c = jnp.sum(log_keep, axis=1, keepdims=True)
                acc_ref[u, h] = accs[h]
                c_ref[u, h] = c
                live = c if live is None else jnp.maximum(live, c)
            o_ref[rows, :] = combine(accs[0], accs[1])

        @pl.when(jnp.max(live) > SB_LOG_UNDERFLOW)
        def _():
            for u, (k_off, _) in enumerate(layout):
                rows = pl.ds(pl.multiple_of((t_block + u) * tq, tq), tq)
                q_heads = split_heads(q_ref[rows, :] * scale)

                def cond(carry):
                    j, alive = carry
                    return jnp.logical_and(j >= 0, alive)

                def walk(carry):
                    j, _ = carry
                    key_rows = pl.ds(pl.multiple_of(j * tq, tq), tq)
                    kb, vb = k_ref[key_rows, :], v_ref[key_rows, :]
                    for h in range(2):
                        log_beta, log_keep = scores(q_heads[h], kb)
                        later = jnp.dot(log_keep.astype(BF16), upper_ref[0:tq, 0:tq],
                                        preferred_element_type=F32)
                        c = c_ref[u, h]
                        w = jnp.exp(log_beta + later + c)
                        acc_ref[u, h] += jnp.dot(w.astype(BF16), vb, preferred_element_type=F32)
                        c_ref[u, h] = c + jnp.sum(log_keep, axis=1, keepdims=True)
                    return j - 1, jnp.max(c_ref[u]) > SB_LOG_UNDERFLOW

                lax.while_loop(cond, walk, (k_block + k_off // tq - 1,
                                            jnp.max(c_ref[u]) > SB_LOG_UNDERFLOW))
                o_ref[rows, :] = combine(acc_ref[u, 0], acc_ref[u, 1])

    back_blocks = back // tq
    head_layout = [(max(u - back_blocks, 0) * tq, min(u, back_blocks) * tq) for u in range(SB_GROUP)]
    group(0, 0, head_layout)

    def body(n, _):
        group(n * SB_GROUP, n * SB_GROUP - back_blocks, [(u * tq, back) for u in range(SB_GROUP)])
        return 0

    lax.fori_loop(1, seq // (SB_GROUP * tq), body, 0)


def _stick_breaking(proj, bsz, seq, n_heads, q_col, k_col, v_col):
    t = proj.shape[0]
    n_pairs = n_heads // 2
    qb, kb, vb = q_col // PAIR, k_col // PAIR, v_col // PAIR
    span = SB_LOOKBACK + SB_BLOCK
    upper = (jnp.arange(span)[:, None] > jnp.arange(span)[None, :]).astype(BF16)
    seq_spec = lambda off: pl.BlockSpec((seq, PAIR), lambda b, p: (b, off + p))
    return pl.pallas_call(
        functools.partial(_sb_kernel, seq=seq),
        out_shape=jax.ShapeDtypeStruct((t, n_heads * HEAD_DIM), BF16),
        grid=(bsz, n_pairs),
        in_specs=[seq_spec(qb), seq_spec(kb), seq_spec(vb),
                  pl.BlockSpec((span, span), lambda b, p: (0, 0))],
        out_specs=pl.BlockSpec((seq, PAIR), lambda b, p: (b, p)),
        scratch_shapes=[pltpu.VMEM((SB_GROUP, 2, SB_BLOCK, PAIR), F32),
                        pltpu.VMEM((SB_GROUP, 2, SB_BLOCK, 1), F32)],
        compiler_params=_cparams(("parallel", "parallel")),
        name="stick_breaking_attention",
    )(proj, proj, proj, upper)


def _dilated_kernel(slope_ref, q_ref, k_ref, v_ref, o_ref, num_ref, den_ref, max_ref, *, seq, tq,
                    n_sub):
    pair = pl.program_id(1)
    first = _head_masks(tq)
    span = 2 * tq
    group = n_sub * tq
    first_keys = _head_masks(group + tq)
    ones = jnp.ones((group + tq, PAIR), BF16)
    row = lax.broadcasted_iota(jnp.int32, (tq, span), 0)
    col = lax.broadcasted_iota(jnp.int32, (tq, span), 1)
    scale = HEAD_DIM ** -0.5

    for branch, (window, r) in enumerate(sorted(DILATED_PATTERNS, key=lambda wr: -wr[1])):
        assert window // r == tq
        per_class = seq // (r * group)
        stride = None if r == 1 else r

        def bias(h, class_start):
            hops = tq + row - col
            valid = jnp.logical_and(hops >= 0, hops <= tq)
            if class_start:
                valid = jnp.logical_and(valid, col >= tq)
            return jnp.where(valid, -(slope_ref[2 * pair + h] * (hops * r).astype(F32)), NEG_INF)

        bias_rest = [bias(h, False) for h in range(2)]
        bias_head = [bias(h, True) for h in range(2)]

        def body(n, _):
            cls = n // per_class
            i = n - cls * per_class
            m0 = i * group
            rows = pl.ds(cls + r * m0, group, stride=stride)
            before = pl.ds(cls + r * jnp.maximum(m0 - tq, 0), tq, stride=stride)
            q_all = (q_ref[rows, :] * scale).astype(BF16)
            k_all = jnp.concatenate([k_ref[before, :], k_ref[rows, :]], axis=0).astype(BF16)
            v_all = jnp.concatenate([v_ref[before, :], v_ref[rows, :]], axis=0).astype(BF16)
            v_heads = (jnp.where(first_keys, v_all, ones), jnp.where(first_keys, ones, v_all))
            nums, dens, maxes = [], [], []
            for u in range(n_sub):
                q = q_all[u * tq:(u + 1) * tq]
                kc = k_all[u * tq:u * tq + span]
                q_heads = (jnp.where(first, q, jnp.zeros_like(q)),
                           jnp.where(first, jnp.zeros_like(q), q))
                res, mxs = [], []
                for h in range(2):
                    z = lax.dot_general(q_heads[h], kc, (((1,), (1,)), ((), ())),
                                        preferred_element_type=F32)
                    b = bias_rest[h] if u else jnp.where(i == 0, bias_head[h], bias_rest[h])
                    logits = z + b
                    m = jnp.max(logits, axis=-1, keepdims=True)
                    p = jnp.exp(logits - m)
                    res.append(jnp.dot(p.astype(BF16), v_heads[h][u * tq:u * tq + span],
                                       preferred_element_type=F32))
                    mxs.append(jnp.broadcast_to(m, (tq, PAIR)))
                nums.append(jnp.where(first, res[0], res[1]))
                dens.append(pltpu.roll(jnp.where(first, res[1], res[0]), HEAD_DIM, 1))
                maxes.append(jnp.where(first, mxs[0], mxs[1]))
            num = jnp.concatenate(nums, axis=0)
            den = jnp.concatenate(dens, axis=0)
            mx = jnp.concatenate(maxes, axis=0)
            if branch == 0:
                num_ref[rows, :] = num
                den_ref[rows, :] = den
                max_ref[rows, :] = mx
            else:
                m_old = max_ref[rows, :]
                m_new = jnp.maximum(m_old, mx)
                w_old = jnp.exp(m_old - m_new)
                w_new = jnp.exp(mx - m_new)
                num_ref[rows, :] = w_old * num_ref[rows, :] + w_new * num
                den_ref[rows, :] = w_old * den_ref[rows, :] + w_new * den
                max_ref[rows, :] = m_new
            return 0

        lax.fori_loop(0, seq // group, body, 0, unroll=2)

    o_ref[...] = (num_ref[...] / den_ref[...]).astype(o_ref.dtype)


def _dilated_attention(qkv, slopes, bsz, seq, n_heads):
    t = qkv.shape[0]
    n_pairs = n_heads // 2
    tq = DILATED_PATTERNS[0][0] // DILATED_PATTERNS[0][1]
    shortest_class = seq // max(r for _, r in DILATED_PATTERNS)
    n_sub = min(DILATED_GROUP, shortest_class // tq)
    spec = lambda off: pl.BlockSpec((seq, PAIR), lambda b, p: (b, off + p))
    return pl.pallas_call(
        functools.partial(_dilated_kernel, seq=seq, tq=tq, n_sub=n_sub),
        out_shape=jax.ShapeDtypeStruct((t, n_heads * HEAD_DIM), BF16),
        grid=(bsz, n_pairs),
        in_specs=[pl.BlockSpec(memory_space=pltpu.SMEM), spec(0), spec(n_pairs), spec(2 * n_pairs)],
        out_specs=pl.BlockSpec((seq, PAIR), lambda b, p: (b, p)),
        scratch_shapes=[pltpu.VMEM((seq, PAIR), F32)] * 3,
        compiler_params=_cparams(("parallel", "parallel")),
        name="dilated_window_attention",
    )(slopes, qkv, qkv, qkv)


def _logf_cumsum_kernel(f_ref, b_ref, o_ref, floor_ref, carry_ref, low_ref, *, tb):
    i = pl.program_id(1)

    @pl.when(i == 0)
    def _():
        carry_ref[...] = jnp.zeros_like(carry_ref)
        low_ref[...] = jnp.zeros_like(low_ref)

    x = f_ref[...] + b_ref[...]
    log_f = jnp.minimum(x, 0.0) - jnp.log1p(jnp.exp(-jnp.abs(x)))
    row = lax.broadcasted_iota(jnp.int32, (tb, tb), 0)
    col = lax.broadcasted_iota(jnp.int32, (tb, tb), 1)
    lower = jnp.where(col <= row, 1.0, 0.0).astype(BF16)
    cum = _split_dot(lower, log_f, 3) + carry_ref[...]
    o_ref[...] = cum
    carry_ref[...] = cum[tb - 1:tb, :]
    low = jnp.minimum(low_ref[...], jnp.min(cum, axis=0, keepdims=True))
    low_ref[...] = low
    floor_ref[pl.ds(i, 1), :] = low


def _logf_cumsum(rest, b_f_row, bsz, seq, col_block, tb):
    t = rest.shape[0]
    nb = seq // tb
    return pl.pallas_call(
        functools.partial(_logf_cumsum_kernel, tb=tb),
        out_shape=(jax.ShapeDtypeStruct((t, LANES), F32),
                   jax.ShapeDtypeStruct((bsz, nb, LANES), F32)),
        grid=(bsz, nb),
        in_specs=[pl.BlockSpec((tb, LANES), lambda b, i: (b * nb + i, col_block)),
                  pl.BlockSpec((1, LANES), lambda b, i: (0, 0))],
        out_specs=(pl.BlockSpec((tb, LANES), lambda b, i: (b * nb + i, 0)),
                   pl.BlockSpec((None, nb, LANES), lambda b, i: (b, 0, 0))),
        scratch_shapes=[pltpu.VMEM((1, LANES), F32), pltpu.VMEM((1, LANES), F32)],
        compiler_params=_cparams(("parallel", "arbitrary")),
        name="log_forget_cumsum",
    )(rest, b_f_row)


FOX_BLOCK = 1024
FOX_F_PARTS = 3


def _fox_kernel(q_ref, k_ref, v_ref, f_ref, floor_ref, o_ref, acc_ref, m_ref, ka_ref, knorm_ref,
                *, tq, seq):
    pair = pl.program_id(1)
    qi = pl.program_id(2)
    reps = tq // LANES
    first = _head_masks(tq)
    lane = lax.broadcasted_iota(jnp.int32, (tq, PAIR), 1)
    f_base = (HEAD_DIM, 0)
    f_lanes = [jnp.logical_and(lane >= f_base[h], lane < f_base[h] + FOX_F_PARTS) for h in range(2)]

    @pl.when(qi == 0)
    def _():
        src = lax.broadcasted_iota(jnp.int32, (LANES, PAIR), 0)
        dst = lax.broadcasted_iota(jnp.int32, (LANES, PAIR), 1)
        place = [[jnp.where(jnp.logical_and(src == 2 * pair + h, dst == f_base[h] + t), 1.0, 0.0)
                  .astype(BF16) for t in range(FOX_F_PARTS)] for h in range(2)]

        def prepare(c, carry):
            rows = pl.ds(pl.multiple_of(c * tq, tq), tq)
            kb = k_ref[rows, :]
            parts = _split_bf16(f_ref[rows, :], FOX_F_PARTS)
            for h in range(2):
                f_terms = None
                for t in range(FOX_F_PARTS):
                    d = jnp.dot(parts[t], place[h][t], preferred_element_type=F32)
                    f_terms = d if f_terms is None else f_terms + d
                own = first if h == 0 else jnp.logical_not(first)
                ka_ref[h, rows, :] = jnp.where(own, kb, (-f_terms).astype(BF16))
            sq = kb.astype(F32)
            sq = sq * sq
            n0 = jnp.sum(jnp.where(first, sq, 0.0), axis=1, keepdims=True)
            n1 = jnp.sum(jnp.where(first, 0.0, sq), axis=1, keepdims=True)
            return jnp.maximum(carry[0], n0), jnp.maximum(carry[1], n1)

        zero = jnp.zeros((tq, 1), F32)
        n0, n1 = lax.fori_loop(0, seq // tq, prepare, (zero, zero))
        knorm_ref[0] = jnp.sqrt(jnp.max(n0))
        knorm_ref[1] = jnp.sqrt(jnp.max(n1))

    q = q_ref[...] * jnp.asarray(HEAD_DIM ** -0.5, BF16)
    zeros = jnp.zeros_like(q)
    q_heads = (jnp.where(first, q, zeros), jnp.where(first, zeros, q))
    one = jnp.ones_like(q)
    q_aug = [jnp.where(f_lanes[h], one, q_heads[h]) for h in range(2)]
    row = lax.broadcasted_iota(jnp.int32, (tq, tq), 0)
    col = lax.broadcasted_iota(jnp.int32, (tq, tq), 1)
    causal = col <= row
    ones = jnp.ones((tq, PAIR), BF16)

    acc_ref[...] = jnp.zeros_like(acc_ref)
    m_ref[...] = jnp.full_like(m_ref, NEG_INF)

    def step(j, masked):
        rows = pl.ds(pl.multiple_of(j * tq, tq), tq)
        vb = v_ref[rows, :]
        v_heads = (jnp.where(first, vb, ones), jnp.where(first, ones, vb))
        for h in range(2):
            s = lax.dot_general(q_aug[h], ka_ref[h, rows, :], (((1,), (1,)), ((), ())),
                                preferred_element_type=F32)
            if masked:
                s = jnp.where(causal, s, NEG_INF)
            m_old = m_ref[h]
            m_new = jnp.maximum(m_old, jnp.max(s, axis=-1, keepdims=True))
            alpha = jnp.exp(m_old - m_new)
            p = jnp.exp(s - jnp.tile(m_new, (1, reps)))
            acc_ref[h] = alpha * acc_ref[h] + jnp.dot(p.astype(BF16), v_heads[h],
                                                      preferred_element_type=F32)
            m_ref[h] = m_new

    step(qi, True)

    floors = floor_ref[...]
    blk = lax.broadcasted_iota(jnp.int32, floors.shape, 0)
    head_lane = lax.broadcasted_iota(jnp.int32, floors.shape, 1)
    skip = []
    for h in range(2):
        q32 = q_heads[h].astype(F32)
        q_norm = jnp.sqrt(jnp.sum(q32 * q32, axis=1, keepdims=True))
        reach = q_norm * (knorm_ref[h] * FOX_NORM_SLACK) - m_ref[h]
        limit = jnp.max(reach) + FOX_LOG_UNDERFLOW
        dead = jnp.logical_and(jnp.logical_and(head_lane == 2 * pair + h, blk < qi), floors > limit)
        skip.append(jnp.sum(jnp.where(dead, 1, 0)))
    start = jnp.minimum(skip[0], skip[1])

    def body(j, _):
        step(j, False)
        return 0

    lax.fori_loop(start, qi, body, 0)
    outs = [acc_ref[h] / pltpu.roll(acc_ref[h], HEAD_DIM, 1) for h in range(2)]
    o_ref[...] = jnp.where(first, outs[0], outs[1]).astype(o_ref.dtype)


def _forgetting_attention(qkv, f_cum, f_floor, bsz, seq, n_heads):
    t = qkv.shape[0]
    tq = FOX_BLOCK
    n_pairs = n_heads // 2
    nq = seq // tq
    return pl.pallas_call(
        functools.partial(_fox_kernel, tq=tq, seq=seq),
        out_shape=jax.ShapeDtypeStruct((t, n_heads * HEAD_DIM), BF16),
        grid=(bsz, n_pairs, nq),
        in_specs=[pl.BlockSpec((tq, PAIR), lambda b, p, i: (b * nq + i, p)),
                  pl.BlockSpec((seq, PAIR), lambda b, p, i: (b, n_pairs + p)),
                  pl.BlockSpec((seq, PAIR), lambda b, p, i: (b, 2 * n_pairs + p)),
                  pl.BlockSpec((seq, LANES), lambda b, p, i: (b, 0)),
                  pl.BlockSpec((None, nq, LANES), lambda b, p, i: (b, 0, 0))],
        out_specs=pl.BlockSpec((tq, PAIR), lambda b, p, i: (b * nq + i, p)),
        scratch_shapes=[pltpu.VMEM((2, tq, PAIR), F32), pltpu.VMEM((2, tq, LANES), F32),
                        pltpu.VMEM((2, seq, PAIR), BF16), pltpu.SMEM((2,), F32)],
        compiler_params=_cparams(("arbitrary", "arbitrary", "arbitrary")),
        name="forgetting_attention",
    )(qkv, qkv, qkv, f_cum, f_floor)


CONV_TAIL = 8


def _ssd_kernel(*refs, n_heads, d_inner, n_xbc):
    xbc_refs = refs[:n_xbc]
    (z_ref, dt_ref, dtt_ref, bias_row_ref, bias_col_ref, alog_row_ref, alog_col_ref, convw_ref,
     convb_ref, dskip_ref, gate_ref, o_ref, state_ref, xc_ref, tail_ref) = refs[n_xbc:]
    q = SSD_CHUNK
    n = SSM_STATE
    hpg = n_heads // SSM_GROUPS
    gw = hpg * HEAD_DIM

    @pl.when(pl.program_id(1) == 0)
    def _():
        state_ref[...] = jnp.zeros_like(state_ref)
        tail_ref[...] = jnp.zeros_like(tail_ref)

    width = convw_ref.shape[0]
    x_raw = jnp.concatenate([r[...] for r in xbc_refs], axis=1)
    xx = jnp.concatenate([tail_ref[...], x_raw], axis=0)
    taps = convw_ref[...]
    y = convb_ref[...] + taps[width - 1:width, :] * x_raw
    for tap in range(width - 1):
        start = CONV_TAIL - (width - 1 - tap)
        y = y + taps[tap:tap + 1, :] * xx[start:start + q, :]
    xc_ref[...] = y * (1.0 / (1.0 + jnp.exp(-y)))
    tail_ref[...] = x_raw[q - CONV_TAIL:q, :]

    row = lax.broadcasted_iota(jnp.int32, (q, q), 0)
    col = lax.broadcasted_iota(jnp.int32, (q, q), 1)
    causal = col <= row
    lower = jnp.where(causal, 1.0, 0.0).astype(BF16)
    upper = jnp.where(row <= col, 1.0, 0.0).astype(BF16)
    er = lax.broadcasted_iota(jnp.int32, (LANES, d_inner), 0)
    ec = lax.broadcasted_iota(jnp.int32, (LANES, d_inner), 1)
    expand = jnp.where(ec // HEAD_DIM == er, 1.0, 0.0).astype(BF16)
    first = _head_masks(q)

    dt = _softplus(dt_ref[...] + bias_row_ref[...])
    a_cum = _split_dot(lower, dt * (-jnp.exp(alog_row_ref[...])), 3)
    a_last = a_cum[q - 1:q, :]
    dt_t = _softplus(dtt_ref[...] + bias_col_ref[...])
    a_cum_t = _dot_split(dt_t * (-jnp.exp(alog_col_ref[...])), upper, 3)

    dt_x = _dot_split(dt, expand, 2)
    decay_in_x = _dot_split(jnp.exp(a_cum), expand, 2)
    decay_out_x = _dot_split(jnp.exp(a_last - a_cum), expand, 2)
    chunk_decay_x = _dot_split(jnp.broadcast_to(jnp.exp(a_last), (8, LANES)), expand, 2)[0:1, :]

    xs = xc_ref[:, 0:d_inner]
    xdt = xs * dt_x
    xdt_bf = xdt.astype(BF16)
    xend_bf = (xdt * decay_out_x).astype(BF16)

    y_groups = []
    for g in range(SSM_GROUPS):
        b_g = xc_ref[:, d_inner + g * n:d_inner + (g + 1) * n].astype(BF16)
        c_g = xc_ref[:, d_inner + (SSM_GROUPS + g) * n:d_inner + (SSM_GROUPS + g + 1) * n].astype(BF16)
        cb = lax.dot_general(c_g, b_g, (((1,), (1,)), ((), ())), preferred_element_type=F32)
        state = state_ref[g]
        y_off = jnp.dot(c_g, state.astype(BF16), preferred_element_type=F32)
        y_g = y_off * decay_in_x[:, g * gw:(g + 1) * gw]
        diag_pairs = []
        for pr in range(hpg // 2):
            lo = g * gw + pr * PAIR
            x_pair = xdt_bf[:, lo:lo + PAIR]
            acc = None
            for hh in range(2):
                h = g * hpg + pr * 2 + hh
                seg = a_cum[:, h:h + 1] - a_cum_t[h:h + 1, :]
                m = (cb * jnp.where(causal, jnp.exp(seg), 0.0)).astype(BF16)
                x_h = jnp.where(first, x_pair, jnp.zeros_like(x_pair)) if hh == 0 else \
                    jnp.where(first, jnp.zeros_like(x_pair), x_pair)
                d = jnp.dot(m, x_h, preferred_element_type=F32)
                acc = d if acc is None else acc + d
            diag_pairs.append(acc)
        y_groups.append(y_g + jnp.concatenate(diag_pairs, axis=1))
        new_state = lax.dot_general(b_g, xend_bf[:, g * gw:(g + 1) * gw], (((0,), (0,)), ((), ())),
                                    preferred_element_type=F32)
        state_ref[g] = state * chunk_decay_x[:, g * gw:(g + 1) * gw] + new_state

    y = jnp.concatenate(y_groups, axis=1) + dskip_ref[...] * xs
    z = z_ref[...]
    gated = y * (z * (1.0 / (1.0 + jnp.exp(-z))))
    gn = d_inner // SSM_GROUPS
    normed = []
    for g in range(SSM_GROUPS):
        gg = gated[:, g * gn:(g + 1) * gn]
        normed.append(gg * lax.rsqrt(jnp.mean(gg * gg, axis=-1, keepdims=True) + NORM_EPS))
    o_ref[...] = (jnp.concatenate(normed, axis=1) * gate_ref[...]).astype(o_ref.dtype)


def _ssd(rest, dt_t, dt_bias, a_log, conv_w, conv_b, d_skip, gate_norm, bsz, seq, n_heads, d_inner,
         z_col, xbc_col, dt_col_block):
    t = rest.shape[0]
    nc = seq // SSD_CHUNK
    width, conv_ch = conv_w.shape
    pad = LANES - n_heads
    row = lambda v: jnp.pad(v, (0, pad)).reshape(1, LANES)
    colv = lambda v: v.reshape(n_heads, 1)
    zb = z_col // d_inner
    tc = math.gcd(xbc_col, conv_ch)
    n_xbc = conv_ch // tc
    const = lambda shape: pl.BlockSpec(shape, lambda b, c: (0,) * len(shape))
    chunk = lambda cols, blk: pl.BlockSpec((SSD_CHUNK, cols), lambda b, c: (b * nc + c, blk))
    return pl.pallas_call(
        functools.partial(_ssd_kernel, n_heads=n_heads, d_inner=d_inner, n_xbc=n_xbc),
        out_shape=jax.ShapeDtypeStruct((t, d_inner), BF16),
        grid=(bsz, nc),
        in_specs=[chunk(tc, xbc_col // tc + j) for j in range(n_xbc)] + [
            chunk(d_inner, zb), chunk(LANES, dt_col_block),
            pl.BlockSpec((None, n_heads, SSD_CHUNK), lambda b, c: (b, 0, c)),
            const((1, LANES)), const((n_heads, 1)), const((1, LANES)), const((n_heads, 1)),
            const((width, conv_ch)), const((1, conv_ch)), const((1, d_inner)), const((1, d_inner))],
        out_specs=pl.BlockSpec((SSD_CHUNK, d_inner), lambda b, c: (b * nc + c, 0)),
        scratch_shapes=[pltpu.VMEM((SSM_GROUPS, SSM_STATE, d_inner // SSM_GROUPS), F32),
                        pltpu.VMEM((SSD_CHUNK, conv_ch), F32), pltpu.VMEM((CONV_TAIL, conv_ch), F32)],
        compiler_params=_cparams(("parallel", "arbitrary")),
        name="ssd_chunk_scan",
    )(*([rest] * n_xbc), rest, rest, dt_t, row(dt_bias), colv(dt_bias), row(a_log), colv(a_log),
      conv_w, conv_b.reshape(1, conv_ch), jnp.repeat(d_skip, HEAD_DIM).reshape(1, d_inner),
      gate_norm.reshape(1, d_inner))


def _even_mixer(h, w_in, w_out, layer, post_w, resid, next_w, bsz, seq):
    d = h.shape[1]
    n_heads = d // (2 * HEAD_DIM)
    hw = n_heads * HEAD_DIM
    qkv_a = _matmul_f32_weight(h, w_in, layer, 0, 3 * hw, BF16)
    qkv_b = _matmul_f32_weight(h, w_in, layer, 3 * hw, 3 * hw, F32, tm=1024)
    oa = _stick_breaking(qkv_a, bsz, seq, n_heads, 0, hw, 2 * hw)
    slopes = jnp.exp2(-ALIBI_MAX_EXP * jnp.arange(1, n_heads + 1, dtype=F32) / n_heads)
    ob = _dilated_attention(qkv_b, slopes, bsz, seq, n_heads)
    return _matmul_norm_residual([oa, ob], w_out, layer, post_w, resid, next_w, tk=512)


def _odd_mixer(h, w_in_all, layer, b_f, conv_w, conv_b, dt_bias, a_log, d_skip, gate_norm, w_out,
               post_w, resid, next_w, bsz, seq):
    d = h.shape[1]
    n_heads = d // (2 * HEAD_DIM)
    hw = n_heads * HEAD_DIM
    d_inner = hw
    conv_ch = d_inner + 2 * SSM_GROUPS * SSM_STATE
    pad = LANES - n_heads
    c0 = 3 * hw
    w_in = w_in_all[layer]
    f_w = w_in[:, c0:c0 + n_heads]
    z_w = w_in[:, c0 + n_heads:c0 + n_heads + d_inner]
    xbc_w = w_in[:, c0 + n_heads + d_inner:c0 + n_heads + d_inner + conv_ch]
    dt_w = w_in[:, c0 + n_heads + d_inner + conv_ch:]
    w_rest = jnp.concatenate([z_w, xbc_w, jnp.pad(f_w, ((0, 0), (0, pad))),
                              jnp.pad(dt_w, ((0, 0), (0, pad)))], axis=1).astype(BF16)
    qkv = _matmul(h, w_in[:, :c0].astype(BF16), BF16)
    rest = _matmul(h, w_rest, F32)
    f_block = (d_inner + conv_ch) // LANES
    dt_block = f_block + 1

    f_cum, f_floor = _logf_cumsum(rest, jnp.pad(b_f, (0, pad)).reshape(1, LANES), bsz, seq, f_block,
                                  FOX_BLOCK)
    oc = _forgetting_attention(qkv, f_cum, f_floor, bsz, seq, n_heads)

    dt_col = dt_block * LANES
    dt_t = rest[:, dt_col:dt_col + n_heads].reshape(bsz, seq, n_heads).transpose(0, 2, 1)
    y = _ssd(rest, dt_t, dt_bias, a_log, conv_w, conv_b, d_skip, gate_norm, bsz, seq, n_heads,
             d_inner, 0, d_inner, dt_block)
    return _matmul_norm_residual([oc, y], w_out, layer, post_w, resid, next_w, tk=512)


def _mlp(h, w_up, w_down, layer, post_w, resid, next_w):
    hidden = _matmul_f32_weight(h, w_up, layer, 0, w_up.shape[2], BF16, relu2=True)
    return _matmul_norm_residual([hidden], w_down, layer, post_w, resid, next_w)


def kernel(x, mix_norm_pre, mix_norm_post, mlp_norm_pre, mlp_norm_post, ab_w_in, ab_w_out,
           cd_w_in, cd_b_f, cd_conv_w, cd_conv_b, cd_dt_bias, cd_a_log, cd_d_skip,
           cd_gate_norm, cd_w_out, mlp_w_up, mlp_w_down):
    bsz, seq, d = x.shape
    depth = mix_norm_pre.shape[0]
    xf = x.reshape(bsz * seq, d)
    h = _rmsnorm(xf, mix_norm_pre[0])
    ab_w_out_bf, cd_w_out_bf, w_down_bf = (w.astype(BF16) for w in (ab_w_out, cd_w_out, mlp_w_down))
    for layer in range(depth):
        i = layer // 2
        if layer % 2 == 0:
            xf, h = _even_mixer(h, ab_w_in, ab_w_out_bf, i, mix_norm_post[layer], xf,
                                mlp_norm_pre[layer], bsz, seq)
        else:
            xf, h = _odd_mixer(h, cd_w_in, i, cd_b_f[i], cd_conv_w[i], cd_conv_b[i], cd_dt_bias[i],
                               cd_a_log[i], cd_d_skip[i], cd_gate_norm[i], cd_w_out_bf,
                               mix_norm_post[layer], xf, mlp_norm_pre[layer], bsz, seq)
        next_w = mix_norm_pre[layer + 1] if layer + 1 < depth else None
        out = _mlp(h, mlp_w_up, w_down_bf, layer, mlp_norm_post[layer], xf, next_w)
        xf, h = out if next_w is not None else (out, None)
    return xf.reshape(bsz, seq, d)
```
